```python
import jax, jax.numpy as jnp
from jax import lax
import numpy as np

D_MODEL = 1024
BATCH = 8
SEQ = 4096
DEPTH = 1

HEAD_DIM = 64
A_HEADS = 8
A_Q_RANK = 256
A_V_LATENT = 128
IDX_HEADS = 8
IDX_DIM = 64
B_HEADS = 8
D_FF = 2816
MAX_TOPK = 256
Q_BLOCK = 128
ROPE_THETA = 10000.0
EPS = 1e-6
A_WIDTH = A_HEADS * HEAD_DIM
B_WIDTH = B_HEADS * HEAD_DIM
IN_SPLITS = (A_Q_RANK, HEAD_DIM, A_V_LATENT, IDX_DIM, IDX_HEADS,
             B_WIDTH, B_WIDTH, B_WIDTH, D_MODEL, D_MODEL)
IN_COLS = sum(IN_SPLITS)

kernel_name = 'hybrid_dsa_stickbreak_macaron'


def rms_norm(x, g):
    xf = x.astype(jnp.float32)
    y = xf * lax.rsqrt(jnp.mean(xf * xf, axis=-1, keepdims=True) + EPS)
    return (y * g.astype(jnp.float32)).astype(x.dtype)


def swiglu(h, w_gate, w_up, w_down):
    return (jax.nn.silu(h @ w_gate) * (h @ w_up)) @ w_down


def rope_tables(positions, dim):
    inv_freq = ROPE_THETA ** (-jnp.arange(0, dim, 2, dtype=jnp.float32) / dim)
    ang = positions.astype(jnp.float32)[..., None] * inv_freq
    return jnp.cos(ang), jnp.sin(ang)


def apply_rope(x, cos, sin):
    xf = x.astype(jnp.float32)
    x1, x2 = jnp.split(xf, 2, axis=-1)
    return jnp.concatenate([x1 * cos - x2 * sin, x1 * sin + x2 * cos], axis=-1).astype(x.dtype)


def dsa_attention(q_a, k_a, v_a, q_idx, k_idx, w_idx, w_uv, top_k):
    bsz, seq = k_a.shape[0], k_a.shape[1]
    n_blocks = seq // Q_BLOCK
    key_pos = jnp.arange(seq)

    def block(i):
        q0 = i * Q_BLOCK
        qa = lax.dynamic_slice_in_dim(q_a, q0, Q_BLOCK, axis=1)
        qi = lax.dynamic_slice_in_dim(q_idx, q0, Q_BLOCK, axis=1)
        wi = lax.dynamic_slice_in_dim(w_idx, q0, Q_BLOCK, axis=1)
        t_pos = q0 + jnp.arange(Q_BLOCK)
        dots = jnp.einsum('bthd,bsd->bths', qi.astype(jnp.float32), k_idx.astype(jnp.float32))
        score = jnp.einsum('bths,bth->bts', jax.nn.relu(dots), wi.astype(jnp.float32))
        causal = key_pos[None, :] <= t_pos[:, None]
        score = jnp.where(causal[None], score, -jnp.inf)
        _, idx = lax.top_k(score, top_k)
        valid = idx <= t_pos[None, :, None]
        flat = idx.reshape(bsz, Q_BLOCK * top_k)[:, :, None]
        kg = jnp.take_along_axis(k_a, flat, axis=1).reshape(bsz, Q_BLOCK, top_k, HEAD_DIM)
        vg = jnp.take_along_axis(v_a, flat, axis=1).reshape(bsz, Q_BLOCK, top_k, A_V_LATENT)
        logits = jnp.einsum('bthd,btkd->bthk', qa.astype(jnp.float32), kg.astype(jnp.float32)) * (HEAD_DIM ** -0.5)
        logits = jnp.where(valid[:, :, None, :], logits, -jnp.inf)
        p = jax.nn.softmax(logits, axis=-1).astype(v_a.dtype)
        o_lat = jnp.einsum('bthk,btkc->bthc', p, vg)
        o = jnp.einsum('bthc,hcd->bthd', o_lat, w_uv)
        return o.reshape(bsz, Q_BLOCK, A_WIDTH)

    out = lax.map(block, jnp.arange(n_blocks))
    return out.transpose(1, 0, 2, 3).reshape(bsz, seq, A_WIDTH)


def stick_breaking_attention(q_b, k_b, v_b):
    bsz, seq = q_b.shape[0], q_b.shape[1]
    n_blocks = seq // Q_BLOCK
    key_pos = jnp.arange(seq)

    def block(i):
        q0 = i * Q_BLOCK
        qb = lax.dynamic_slice_in_dim(q_b, q0, Q_BLOCK, axis=1)
        t_pos = q0 + jnp.arange(Q_BLOCK)
        z = jnp.einsum('bthd,bshd->bhts', qb.astype(jnp.float32), k_b.astype(jnp.float32)) * (HEAD_DIM ** -0.5)
        strict = (key_pos[None, :] < t_pos[:, None])[None, None]
        log_not = jnp.where(strict, jax.nn.log_sigmoid(-z), 0.0)
        between = lax.cumsum(log_not, axis=3, reverse=True) - log_not
        log_a = jax.nn.log_sigmoid(z) + between
        a = jnp.where(strict, jnp.exp(log_a), 0.0).astype(v_b.dtype)
        o = jnp.einsum('bhts,bshd->bthd', a, v_b)
        return o.reshape(bsz, Q_BLOCK, B_WIDTH)

    out = lax.map(block, jnp.arange(n_blocks))
    return out.transpose(1, 0, 2, 3).reshape(bsz, seq, B_WIDTH)


def setup_inputs(seed: int = 0) -> dict:
    key = jax.random.key(seed)
    ks = jax.random.split(key, 24)
    f32 = jnp.float32

    def w(k, shape, fan_in):
        return jax.random.normal(k, shape, f32) * (fan_in ** -0.5)

    def gain(k, shape):
        return 1.0 + 0.02 * jax.random.normal(k, shape, f32)

    L = DEPTH
    return {
        'x': jax.random.normal(ks[0], (BATCH, SEQ, D_MODEL), f32),
        'positions': jnp.broadcast_to(jnp.arange(SEQ, dtype=jnp.int32), (BATCH, SEQ)),
        'g_ffn1': gain(ks[1], (L, D_MODEL)),
        'w1_gate': w(ks[2], (L, D_MODEL, D_FF), D_MODEL),
        'w1_up': w(ks[3], (L, D_MODEL, D_FF), D_MODEL),
        'w1_down': w(ks[4], (L, D_FF, D_MODEL), D_FF),
        'g_mix': gain(ks[5], (L, D_MODEL)),
        'w_in': w(ks[6], (L, D_MODEL, IN_COLS), D_MODEL),
        'g_cq': gain(ks[7], (L, A_Q_RANK)),
        'w_uq_a': w(ks[8], (L, A_Q_RANK, A_WIDTH), A_Q_RANK),
        'w_q_idx': w(ks[9], (L, A_Q_RANK, IDX_HEADS * IDX_DIM), A_Q_RANK),
        'g_q_a': gain(ks[10], (L, HEAD_DIM)),
        'g_k_a': gain(ks[11], (L, HEAD_DIM)),
        'w_uv_a': w(ks[12], (L, A_HEADS, A_V_LATENT, HEAD_DIM), A_V_LATENT),
        'w_o_a': w(ks[13], (L, A_WIDTH, D_MODEL), A_WIDTH),
        'w_o_b': w(ks[14], (L, B_WIDTH, D_MODEL), B_WIDTH),
        'w_out': w(ks[15], (L, D_MODEL, D_MODEL), D_MODEL),
        'g_ffn2': gain(ks[16], (L, D_MODEL)),
        'w2_gate': w(ks[17], (L, D_MODEL, D_FF), D_MODEL),
        'w2_up': w(ks[18], (L, D_MODEL, D_FF), D_MODEL),
        'w2_down': w(ks[19], (L, D_FF, D_MODEL), D_FF),
    }


def reference(x, positions, g_ffn1, w1_gate, w1_up, w1_down, g_mix, w_in, g_cq, w_uq_a, w_q_idx,
              g_q_a, g_k_a, w_uv_a, w_o_a, w_o_b, w_out, g_ffn2, w2_gate, w2_up, w2_down):
    bsz, seq, _ = x.shape
    top_k = min(MAX_TOPK, seq // 4)
    cos, sin = rope_tables(positions, HEAD_DIM)
    cos_h, sin_h = cos[:, :, None, :], sin[:, :, None, :]
    offsets = [int(o) for o in np.cumsum(IN_SPLITS)[:-1]]
    for l in range(DEPTH):
        x = x + 0.5 * swiglu(rms_norm(x, g_ffn1[l]), w1_gate[l], w1_up[l], w1_down[l])

        h = rms_norm(x, g_mix[l])
        proj = h @ w_in[l]
        (c_q, k_a, v_a, k_idx, w_idx, q_b, k_b, v_b, gate_a, gate_b) = jnp.split(proj, offsets, axis=-1)

        c_q = rms_norm(c_q, g_cq[l])
        q_a = (c_q @ w_uq_a[l]).reshape(bsz, seq, A_HEADS, HEAD_DIM)
        q_a = apply_rope(rms_norm(q_a, g_q_a[l]), cos_h, sin_h)
        k_a = apply_rope(rms_norm(k_a, g_k_a[l]), cos, sin)
        q_idx = apply_rope((c_q @ w_q_idx[l]).reshape(bsz, seq, IDX_HEADS, IDX_DIM), cos_h, sin_h)
        k_idx = apply_rope(k_idx, cos, sin)
        w_idx = w_idx * ((IDX_HEADS ** -0.5) * (IDX_DIM ** -0.5))
        y_a = dsa_attention(q_a, k_a, v_a, q_idx, k_idx, w_idx, w_uv_a[l], top_k)

        y_b = stick_breaking_attention(q_b.reshape(bsz, seq, B_HEADS, HEAD_DIM),
                                       k_b.reshape(bsz, seq, B_HEADS, HEAD_DIM),
                                       v_b.reshape(bsz, seq, B_HEADS, HEAD_DIM))

        merged = jax.nn.sigmoid(gate_a) * (y_a @ w_o_a[l]) + jax.nn.sigmoid(gate_b) * (y_b @ w_o_b[l])
        x = x + merged @ w_out[l]

        x = x + 0.5 * swiglu(rms_norm(x, g_ffn2[l]), w2_gate[l], w2_up[l], w2_down[l])
    return x
```

```python
import functools

import numpy as np
import jax
import jax.numpy as jnp
from jax import lax
from jax.experimental import pallas as pl
from jax.experimental.pallas import tpu as pltpu

HEAD_DIM = 64
A_HEADS = 8
A_Q_RANK = 256
A_V_LATENT = 128
IDX_HEADS = 8
IDX_DIM = 64
B_HEADS = 8
MAX_TOPK = 256
ROPE_THETA = 10000.0
EPS = 1e-6
A_WIDTH = A_HEADS * HEAD_DIM
B_WIDTH = B_HEADS * HEAD_DIM
LANES = 128
INT_MIN = -(2 ** 31)
NEG_BIG = -1e30

_MXU_DT = jnp.bfloat16
_VMEM_LIMIT = 56 * 1024 * 1024


def _cparams(sem):
    return pltpu.CompilerParams(dimension_semantics=sem, vmem_limit_bytes=_VMEM_LIMIT)


def _dot(a, b):
    return jnp.dot(a, b, preferred_element_type=jnp.float32)


def _sigmoid(x):
    return 1.0 / (1.0 + jnp.exp(-x))


def _rms(x, g):
    ms = jnp.mean(x * x, axis=-1, keepdims=True)
    return x * lax.rsqrt(ms + EPS) * g


def _ffn_kernel(x_ref, g_ref, wg_ref, wu_ref, wd_ref, o_ref, h_ref, acc_ref):
    f = pl.program_id(1)

    @pl.when(f == 0)
    def _():
        h_ref[...] = _rms(x_ref[...], g_ref[...]).astype(h_ref.dtype)
        acc_ref[...] = jnp.zeros_like(acc_ref)

    h = h_ref[...]
    gate = _dot(h, wg_ref[...])
    up = _dot(h, wu_ref[...])
    act = (gate * _sigmoid(gate) * up).astype(wd_ref.dtype)
    acc_ref[...] += _dot(act, wd_ref[...])

    @pl.when(f == pl.num_programs(1) - 1)
    def _():
        o_ref[...] = x_ref[...] + 0.5 * acc_ref[...]


def _ffn(x, g, wg, wu, wd, tm, tf):
    n, d = x.shape
    dff = wg.shape[1]
    return pl.pallas_call(
        _ffn_kernel,
        grid=(n // tm, dff // tf),
        in_specs=[
            pl.BlockSpec((tm, d), lambda i, f: (i, 0)),
            pl.BlockSpec((1, d), lambda i, f: (0, 0)),
            pl.BlockSpec((d, tf), lambda i, f: (0, f)),
            pl.BlockSpec((d, tf), lambda i, f: (0, f)),
            pl.BlockSpec((tf, d), lambda i, f: (f, 0)),
        ],
        out_specs=pl.BlockSpec((tm, d), lambda i, f: (i, 0)),
        out_shape=jax.ShapeDtypeStruct((n, d), jnp.float32),
        scratch_shapes=[pltpu.VMEM((tm, d), _MXU_DT), pltpu.VMEM((tm, d), jnp.float32)],
        compiler_params=_cparams(("parallel", "arbitrary")),
        name="ffn",
    )(x, g.reshape(1, d), wg, wu, wd)


def _rope(x, cos, sin_signed, first_half):
    n = x.shape[-1]
    half = HEAD_DIM // 2
    fwd = pltpu.roll(x, n - half, 1)
    bwd = pltpu.roll(x, half, 1)
    return x * cos + jnp.where(first_half, fwd, bwd) * sin_signed


def _proj_kernel(x_ref, pos_ref, gmix_ref, wa_ref, ww_ref, wb_ref, gcq_ref, wuq_ref, wqi_ref,
                 gqa_ref, gka_ref, invf_ref, hmean_ref,
                 qa_ref, qi_ref, kk_ref, va_ref, wi_ref, qkvb_ref):
    h = _rms(x_ref[...], gmix_ref[...]).astype(wa_ref.dtype)

    qkvb_ref[...] = _dot(h, wb_ref[...]).astype(qkvb_ref.dtype)
    wi_ref[...] = _dot(h, ww_ref[...]) * ((IDX_HEADS ** -0.5) * (IDX_DIM ** -0.5))

    pa = _dot(h, wa_ref[...])
    c_q = _rms(pa[:, :A_Q_RANK], gcq_ref[...]).astype(wuq_ref.dtype)
    kk = pa[:, A_Q_RANK:A_Q_RANK + LANES]
    va_ref[...] = pa[:, A_Q_RANK + LANES:].astype(va_ref.dtype)

    ang = pos_ref[...].astype(jnp.float32) * invf_ref[...]
    cos = jnp.cos(ang)
    sin = jnp.sin(ang)
    lane = lax.broadcasted_iota(jnp.int32, cos.shape, 1)
    first_half = (lane & (HEAD_DIM // 2)) == 0
    sin_signed = jnp.where(first_half, -sin, sin)
    reps = A_WIDTH // LANES
    cos_w = jnp.concatenate([cos] * reps, axis=1)
    sin_w = jnp.concatenate([sin_signed] * reps, axis=1)
    first_w = (lax.broadcasted_iota(jnp.int32, cos_w.shape, 1) & (HEAD_DIM // 2)) == 0

    is_ka = lane < HEAD_DIM
    ms_k = jnp.sum(jnp.where(is_ka, kk * kk, 0.0), axis=-1, keepdims=True) * (1.0 / HEAD_DIM)
    kscale = jnp.where(is_ka, lax.rsqrt(ms_k + EPS) * gka_ref[...], 1.0)
    kk_ref[...] = _rope(kk * kscale, cos, sin_signed, first_half).astype(kk_ref.dtype)

    q_i = _dot(c_q, wqi_ref[...])
    qi_ref[...] = _rope(q_i, cos_w, sin_w, first_w).astype(qi_ref.dtype)

    q_a = _dot(c_q, wuq_ref[...])
    sq = q_a * q_a
    sq_hi = sq.astype(hmean_ref.dtype)
    sq_lo = (sq - sq_hi.astype(jnp.float32)).astype(hmean_ref.dtype)
    ms_q = _dot(sq_hi, hmean_ref[...]) + _dot(sq_lo, hmean_ref[...])
    q_a = q_a * lax.rsqrt(ms_q + EPS) * gqa_ref[...]
    qa_ref[...] = (_rope(q_a, cos_w, sin_w, first_w) * (HEAD_DIM ** -0.5)).astype(qa_ref.dtype)


def _proj(x1, pos, g_mix, w_a, w_w, w_b, g_cq, w_uq, w_qi, g_qa, g_ka, tm):
    n, d = x1.shape
    dt = _MXU_DT
    inv_freq = ROPE_THETA ** (-np.arange(0, HEAD_DIM, 2, dtype=np.float64) / HEAD_DIM)
    invf = jnp.asarray(np.tile(inv_freq, LANES // (HEAD_DIM // 2)).astype(np.float32)).reshape(1, LANES)
    head_of = np.arange(A_WIDTH) // HEAD_DIM
    hmean = jnp.asarray((head_of[:, None] == head_of[None, :]).astype(np.float32) / HEAD_DIM, dt)
    gqa = jnp.tile(g_qa, A_HEADS).reshape(1, A_WIDTH)
    gka = jnp.tile(g_ka, LANES // HEAD_DIM).reshape(1, LANES)

    def full(a):
        return pl.BlockSpec(a.shape, lambda i: (0,) * a.ndim)

    def rows(w):
        return pl.BlockSpec((tm, w), lambda i: (i, 0))

    args = (x1, pos, g_mix.reshape(1, d), w_a, w_w, w_b, g_cq.reshape(1, A_Q_RANK), w_uq, w_qi,
            gqa, gka, invf, hmean)
    in_specs = [rows(d), rows(1)] + [full(a) for a in args[2:]]
    out_w = (A_WIDTH, A_WIDTH, LANES, A_V_LATENT, LANES, 3 * B_WIDTH)
    out_dt = (dt, dt, dt, dt, jnp.float32, dt)
    return pl.pallas_call(
        _proj_kernel,
        grid=(n // tm,),
        in_specs=in_specs,
        out_specs=[rows(w) for w in out_w],
        out_shape=[jax.ShapeDtypeStruct((n, w), t) for w, t in zip(out_w, out_dt)],
        compiler_params=_cparams(("parallel",)),
        name="proj",
    )(*args)


def _dsa_kernel(qi_ref, w_ref, qa_ref, kit_ref, kat_ref, va_ref, wuv_ref, o_ref, sk_ref, *, tq, tkc, top_k):
    i = pl.program_id(1)
    n_chunks = (i * tq + tq + tkc - 1) // tkc
    t_row = i * tq + lax.broadcasted_iota(jnp.int32, (tq, 1), 0)

    def chunk(c):
        return pl.ds(pl.multiple_of(c * tkc, tkc), tkc)

    qi = qi_ref[0].reshape(IDX_HEADS * tq, IDX_DIM)
    w = w_ref[0]

    def score_body(c, carry):
        d = _dot(qi, kit_ref[0, :, chunk(c)])
        s = jnp.zeros((tq, tkc), jnp.float32)
        for h in range(IDX_HEADS):
            s = s + w[:, h:h + 1] * jnp.maximum(d[h * tq:(h + 1) * tq], 0.0)
        bits = lax.bitcast_convert_type(s, jnp.int32)
        key = bits ^ ((bits >> 31) & 0x7FFFFFFF)
        col = c * tkc + lax.broadcasted_iota(jnp.int32, (tq, tkc), 1)
        sk_ref[:, chunk(c)] = jnp.where(col <= t_row, key, INT_MIN)
        return carry

    lax.fori_loop(0, n_chunks, score_body, 0)

    def count_ge(thr):
        def body(c, part):
            m = (sk_ref[:, chunk(c)] >= thr).astype(jnp.int32)
            for j in range(tkc // LANES):
                part = part + m[:, j * LANES:(j + 1) * LANES]
            return part
        part = lax.fori_loop(0, n_chunks, body, jnp.zeros((tq, LANES), jnp.int32))
        return jnp.sum(part, axis=1, keepdims=True)

    def bit_body(b, t_u):
        cand = t_u | lax.shift_left(jnp.int32(1), 31 - b)
        cnt = count_ge(cand ^ INT_MIN)
        return jnp.where(cnt >= top_k, cand, t_u)

    t_u = lax.fori_loop(0, 32, bit_body, jnp.zeros((tq, 1), jnp.int32))
    thr = jnp.maximum(t_u ^ INT_MIN, INT_MIN + 1)

    qa = qa_ref[0].reshape(A_HEADS * tq, HEAD_DIM)

    def attn_body(c, carry):
        m_run, l_run, acc = carry
        lg = _dot(qa, kat_ref[0, :, chunk(c)])
        sel = sk_ref[:, chunk(c)] >= thr
        v = va_ref[0, chunk(c), :]
        m_new, l_new, ps = [], [], []
        for h in range(A_HEADS):
            lh = jnp.where(sel, lg[h * tq:(h + 1) * tq], NEG_BIG)
            mh = jnp.maximum(m_run[h], jnp.max(lh, axis=-1, keepdims=True))
            ph = jnp.exp(lh - mh)
            alpha = jnp.exp(m_run[h] - mh)
            m_new.append(mh)
            l_new.append(alpha * l_run[h] + jnp.sum(ph, axis=-1, keepdims=True))
            ps.append((alpha, ph.astype(v.dtype)))
        pv = _dot(jnp.concatenate([p for _, p in ps], axis=0), v)
        acc = tuple(a * acc[h] + pv[h * tq:(h + 1) * tq] for h, (a, _) in enumerate(ps))
        return tuple(m_new), tuple(l_new), acc

    init = (tuple(jnp.full((tq, 1), NEG_BIG, jnp.float32) for _ in range(A_HEADS)),
            tuple(jnp.zeros((tq, 1), jnp.float32) for _ in range(A_HEADS)),
            tuple(jnp.zeros((tq, A_V_LATENT), jnp.float32) for _ in range(A_HEADS)))
    _, l_fin, acc = lax.fori_loop(0, n_chunks, attn_body, init)

    outs = []
    for h in range(A_HEADS):
        o_lat = (acc[h] / l_fin[h]).astype(wuv_ref.dtype)
        outs.append(_dot(o_lat, wuv_ref[h]))
    o_ref[0] = jnp.concatenate(outs, axis=1).astype(o_ref.dtype)


def _dsa(qi_h, w_idx, qa_h, kit, kat, va, w_uv, top_k, tq, tkc):
    b, _, s, _ = qi_h.shape
    kern = functools.partial(_dsa_kernel, tq=tq, tkc=tkc, top_k=top_k)
    return pl.pallas_call(
        kern,
        grid=(b, s // tq),
        in_specs=[
            pl.BlockSpec((1, IDX_HEADS, tq, IDX_DIM), lambda bi, i: (bi, 0, i, 0)),
            pl.BlockSpec((1, tq, LANES), lambda bi, i: (bi, i, 0)),
            pl.BlockSpec((1, A_HEADS, tq, HEAD_DIM), lambda bi, i: (bi, 0, i, 0)),
            pl.BlockSpec((1, IDX_DIM, s), lambda bi, i: (bi, 0, 0)),
            pl.BlockSpec((1, HEAD_DIM, s), lambda bi, i: (bi, 0, 0)),
            pl.BlockSpec((1, s, A_V_LATENT), lambda bi, i: (bi, 0, 0)),
            pl.BlockSpec((A_HEADS, A_V_LATENT, HEAD_DIM), lambda bi, i: (0, 0, 0)),
        ],
        out_specs=pl.BlockSpec((1, tq, A_WIDTH), lambda bi, i: (bi, i, 0)),
        out_shape=jax.ShapeDtypeStruct((b, s, A_WIDTH), _MXU_DT),
        scratch_shapes=[pltpu.VMEM((tq, s), jnp.int32)],
        compiler_params=_cparams(("parallel", "arbitrary")),
        name="dsa",
    )(qi_h, w_idx, qa_h, kit, kat, va, w_uv)


def _sb_kernel(q_ref, kt_ref, v_ref, o_ref, *, tb):
    i = pl.program_id(2)
    q = q_ref[0, 0]
    row = lax.broadcasted_iota(jnp.int32, (tb, tb), 0)
    col = lax.broadcasted_iota(jnp.int32, (tb, tb), 1)
    later = (row > col).astype(kt_ref.dtype)
    strict = col < row

    def block(j, carry, acc, diag):
        ks = pl.ds(pl.multiple_of(j * tb, tb), tb)
        z = _dot(q, kt_ref[0, 0, :, ks])
        soft = jnp.log1p(jnp.exp(-jnp.abs(z)))
        log_not = -jnp.maximum(z, 0.0) - soft
        log_sig = jnp.minimum(z, 0.0) - soft
        if diag:
            log_not = jnp.where(strict, log_not, 0.0)
        hi = log_not.astype(later.dtype)
        lo = (log_not - hi.astype(jnp.float32)).astype(later.dtype)
        between = _dot(hi, later) + _dot(lo, later)
        a = jnp.exp(log_sig + between + carry)
        if diag:
            a = jnp.where(strict, a, 0.0)
        acc = acc + _dot(a.astype(v_ref.dtype), v_ref[0, 0, ks, :])
        carry = carry + between[:, 0:1] + log_not[:, 0:1]
        return carry, acc

    carry, acc = block(i, jnp.zeros((tb, 1), jnp.float32), jnp.zeros((tb, HEAD_DIM), jnp.float32), True)

    def body(k, ca):
        return block(i - 1 - k, ca[0], ca[1], False)

    _, acc = lax.fori_loop(0, i, body, (carry, acc))
    o_ref[0, 0] = acc.astype(o_ref.dtype)


def _stickbreak(q_h, kt_h, v_h, tb):
    b, nh, s, _ = q_h.shape
    return pl.pallas_call(
        functools.partial(_sb_kernel, tb=tb),
        grid=(b, nh, s // tb),
        in_specs=[
            pl.BlockSpec((1, 1, tb, HEAD_DIM), lambda bi, h, i: (bi, h, i, 0)),
            pl.BlockSpec((1, 1, HEAD_DIM, s), lambda bi, h, i: (bi, h, 0, 0)),
            pl.BlockSpec((1, 1, s, HEAD_DIM), lambda bi, h, i: (bi, h, 0, 0)),
        ],
        out_specs=pl.BlockSpec((1, 1, tb, HEAD_DIM), lambda bi, h, i: (bi, h, i, 0)),
        out_shape=jax.ShapeDtypeStruct((b, nh, s, HEAD_DIM), _MXU_DT),
        compiler_params=_cparams(("parallel", "parallel", "arbitrary")),
        name="stickbreak",
    )(q_h, kt_h, v_h)


def _merge_kernel(x_ref, ya_ref, yb_ref, gmix_ref, wg_ref, woa_ref, wob_ref, wout_ref, o_ref):
    x = x_ref[...]
    d = x.shape[-1]
    h = _rms(x, gmix_ref[...]).astype(wg_ref.dtype)
    gates = _dot(h, wg_ref[...])
    merged = (_sigmoid(gates[:, :d]) * _dot(ya_ref[...], woa_ref[...])
              + _sigmoid(gates[:, d:]) * _dot(yb_ref[...], wob_ref[...]))
    o_ref[...] = x + _dot(merged.astype(wout_ref.dtype), wout_ref[...])


def _merge(x1, ya, yb, g_mix, w_g, w_oa, w_ob, w_out, tm):
    n, d = x1.shape

    def full(a):
        return pl.BlockSpec(a.shape, lambda i: (0,) * a.ndim)

    def rows(w):
        return pl.BlockSpec((tm, w), lambda i: (i, 0))

    args = (x1, ya, yb, g_mix.reshape(1, d), w_g, w_oa, w_ob, w_out)
    return pl.pallas_call(
        _merge_kernel,
        grid=(n // tm,),
        in_specs=[rows(d), rows(A_WIDTH), rows(B_WIDTH)] + [full(a) for a in args[3:]],
        out_specs=rows(d),
        out_shape=jax.ShapeDtypeStruct((n, d), jnp.float32),
        compiler_params=_cparams(("parallel",)),
        name="merge",
    )(*args)


def _pick(n, prefs):
    for p in prefs:
        if n % p == 0:
            return p
    return n


def _layer(x, positions, g_ffn1, w1_gate, w1_up, w1_down, g_mix, w_in, g_cq, w_uq_a, w_q_idx,
           g_q_a, g_k_a, w_uv_a, w_o_a, w_o_b, w_out, g_ffn2, w2_gate, w2_up, w2_down):
    bsz, seq, d = x.shape
    n = bsz * seq
    dt = _MXU_DT
    top_k = min(MAX_TOPK, seq // 4)
    tm_ffn = _pick(n, (1024, 512, 256))
    tf = _pick(w1_gate.shape[1], (256, 128))
    tm = _pick(n, (512, 256))
    tq = 128
    tkc = _pick(seq, (512, 256, 128))
    tb = _pick(seq, (256, 128))

    xf = x.reshape(n, d)
    x1 = _ffn(xf, g_ffn1, w1_gate.astype(dt), w1_up.astype(dt), w1_down.astype(dt), tm_ffn, tf)

    o = np.cumsum((A_Q_RANK, HEAD_DIM, A_V_LATENT, IDX_DIM, IDX_HEADS, B_WIDTH, B_WIDTH, B_WIDTH, d, d))
    w_a = jnp.concatenate([w_in[:, :o[0]], w_in[:, o[0]:o[1]], w_in[:, o[2]:o[3]], w_in[:, o[1]:o[2]]],
                          axis=1).astype(dt)
    w_w = jnp.pad(w_in[:, o[3]:o[4]], ((0, 0), (0, LANES - IDX_HEADS))).astype(dt)
    w_b = jnp.concatenate([w_in[:, o[4]:o[5]] * (HEAD_DIM ** -0.5), w_in[:, o[5]:o[7]]], axis=1).astype(dt)
    w_g = w_in[:, o[7]:].astype(dt)

    qa, qi, kk, va, widx, qkvb = _proj(
        x1, positions.reshape(n, 1), g_mix, w_a, w_w, w_b, g_cq, w_uq_a.astype(dt), w_q_idx.astype(dt),
        g_q_a, g_k_a, tm)

    def heads(a, nh):
        return a.reshape(bsz, seq, nh, HEAD_DIM).transpose(0, 2, 1, 3)

    kk = kk.reshape(bsz, seq, LANES)
    kat = kk[:, :, :HEAD_DIM].transpose(0, 2, 1)
    kit = kk[:, :, HEAD_DIM:].transpose(0, 2, 1)
    ya = _dsa(heads(qi, IDX_HEADS), widx.reshape(bsz, seq, LANES), heads(qa, A_HEADS), kit, kat,
              va.reshape(bsz, seq, A_V_LATENT), w_uv_a.astype(dt), top_k, tq, tkc)

    q_b = heads(qkvb[:, :B_WIDTH], B_HEADS)
    kt_b = heads(qkvb[:, B_WIDTH:2 * B_WIDTH], B_HEADS).transpose(0, 1, 3, 2)
    v_b = heads(qkvb[:, 2 * B_WIDTH:], B_HEADS)
    yb = _stickbreak(q_b, kt_b, v_b, tb).transpose(0, 2, 1, 3).reshape(n, B_WIDTH)

    x2 = _merge(x1, ya.reshape(n, A_WIDTH), yb, g_mix, w_g, w_o_a.astype(dt), w_o_b.astype(dt),
                w_out.astype(dt), tm)
    x3 = _ffn(x2, g_ffn2, w2_gate.astype(dt), w2_up.astype(dt), w2_down.astype(dt), tm_ffn, tf)
    return x3.reshape(bsz, seq, d)


def kernel(x, positions, g_ffn1, w1_gate, w1_up, w1_down, g_mix, w_in, g_cq, w_uq_a, w_q_idx, g_q_a, g_k_a,
           w_uv_a, w_o_a, w_o_b, w_out, g_ffn2, w2_gate, w2_up, w2_down):
    depth = g_ffn1.shape[0]
    for l in range(depth):
        x = _layer(x, positions, g_ffn1[l], w1_gate[l], w1_up[l], w1_down[l], g_mix[l], w_in[l], g_cq[l],
                   w_uq_a[l], w_q_idx[l], g_q_a[l], g_k_a[l], w_uv_a[l], w_o_a[l], w_o_b[l], w_out[l],
                   g_ffn2[l], w2_gate[l], w2_up[l], w2_down[l])
    return x
```

```python
import functools

import numpy as np
import jax
import jax.numpy as jnp
from jax import lax
from jax.experimental import pallas as pl
from jax.experimental.pallas import tpu as pltpu

HEAD_DIM = 64
A_HEADS = 8
A_Q_RANK = 256
A_V_LATENT = 128
IDX_HEADS = 8
IDX_DIM = 64
B_HEADS = 8
MAX_TOPK = 256
ROPE_THETA = 10000.0
EPS = 1e-6
A_WIDTH = A_HEADS * HEAD_DIM
B_WIDTH = B_HEADS * HEAD_DIM
LANES = 128
INT_MIN = -(2 ** 31)
NEG_BIG = -1e30
LOG2E = 1.4426950408889634

_MXU_DT = jnp.bfloat16
_VMEM_LIMIT = 56 * 1024 * 1024


def _cparams(sem):
    return pltpu.CompilerParams(dimension_semantics=sem, vmem_limit_bytes=_VMEM_LIMIT)


def _dot(a, b):
    return jnp.dot(a, b, preferred_element_type=jnp.float32)


def _sigmoid(x):
    return 1.0 / (1.0 + jnp.exp(-x))


def _rms(x, g):
    ms = jnp.mean(x * x, axis=-1, keepdims=True)
    return x * lax.rsqrt(ms + EPS) * g


def _ffn_kernel(x_ref, g_ref, wg_ref, wu_ref, wd_ref, o_ref, h_ref, acc_ref):
    f = pl.program_id(1)

    @pl.when(f == 0)
    def _():
        h_ref[...] = _rms(x_ref[...], g_ref[...]).astype(h_ref.dtype)
        acc_ref[...] = jnp.zeros_like(acc_ref)

    h = h_ref[...]
    gate = _dot(h, wg_ref[...])
    up = _dot(h, wu_ref[...])
    act = (gate * _sigmoid(gate) * up).astype(wd_ref.dtype)
    acc_ref[...] += _dot(act, wd_ref[...])

    @pl.when(f == pl.num_programs(1) - 1)
    def _():
        o_ref[...] = x_ref[...] + 0.5 * acc_ref[...]


def _ffn(x, g, wg, wu, wd, tm, tf):
    n, d = x.shape
    dff = wg.shape[1]
    return pl.pallas_call(
        _ffn_kernel,
        grid=(n // tm, dff // tf),
        in_specs=[
            pl.BlockSpec((tm, d), lambda i, f: (i, 0)),
            pl.BlockSpec((1, d), lambda i, f: (0, 0)),
            pl.BlockSpec((d, tf), lambda i, f: (0, f)),
            pl.BlockSpec((d, tf), lambda i, f: (0, f)),
            pl.BlockSpec((tf, d), lambda i, f: (f, 0)),
        ],
        out_specs=pl.BlockSpec((tm, d), lambda i, f: (i, 0)),
        out_shape=jax.ShapeDtypeStruct((n, d), jnp.float32),
        scratch_shapes=[pltpu.VMEM((tm, d), _MXU_DT), pltpu.VMEM((tm, d), jnp.float32)],
        compiler_params=_cparams(("parallel", "arbitrary")),
        name="ffn",
    )(x, g.reshape(1, d), wg, wu, wd)


def _rope(x, cos, sin_signed, first_half):
    n = x.shape[-1]
    half = HEAD_DIM // 2
    fwd = pltpu.roll(x, n - half, 1)
    bwd = pltpu.roll(x, half, 1)
    return x * cos + jnp.where(first_half, fwd, bwd) * sin_signed


def _proj_kernel(x_ref, pos_ref, gmix_ref, wa_ref, ww_ref, wb_ref, gcq_ref, wuq_ref, wqi_ref,
                 gqa_ref, gka_ref, invf_ref, hmean_ref,
                 qa_ref, qi_ref, kat_ref, kit_ref, va_ref, wi_ref, qv_ref, kbt_ref):
    h = _rms(x_ref[...], gmix_ref[...]).astype(wa_ref.dtype)

    r = _dot(h, wb_ref[...])
    qv_ref[:, :B_WIDTH] = r[:, :B_WIDTH].astype(qv_ref.dtype)
    qv_ref[:, B_WIDTH:] = r[:, 2 * B_WIDTH:].astype(qv_ref.dtype)
    kbt_ref[...] = r[:, B_WIDTH:2 * B_WIDTH].T.astype(kbt_ref.dtype)
    wi_ref[...] = _dot(h, ww_ref[...]) * ((IDX_HEADS ** -0.5) * (IDX_DIM ** -0.5))

    pa = _dot(h, wa_ref[...])
    c_q = _rms(pa[:, :A_Q_RANK], gcq_ref[...]).astype(wuq_ref.dtype)
    kk = pa[:, A_Q_RANK:A_Q_RANK + LANES]
    va_ref[...] = pa[:, A_Q_RANK + LANES:].astype(va_ref.dtype)

    ang = pos_ref[...].astype(jnp.float32) * invf_ref[...]
    cos = jnp.cos(ang)
    sin = jnp.sin(ang)
    lane = lax.broadcasted_iota(jnp.int32, cos.shape, 1)
    first_half = (lane & (HEAD_DIM // 2)) == 0
    sin_signed = jnp.where(first_half, -sin, sin)
    reps = A_WIDTH // LANES
    cos_w = jnp.concatenate([cos] * reps, axis=1)
    sin_w = jnp.concatenate([sin_signed] * reps, axis=1)
    first_w = (lax.broadcasted_iota(jnp.int32, cos_w.shape, 1) & (HEAD_DIM // 2)) == 0

    is_ka = lane < HEAD_DIM
    ms_k = jnp.sum(jnp.where(is_ka, kk * kk, 0.0), axis=-1, keepdims=True) * (1.0 / HEAD_DIM)
    kscale = jnp.where(is_ka, lax.rsqrt(ms_k + EPS) * gka_ref[...], 1.0)
    kk = _rope(kk * kscale, cos, sin_signed, first_half)
    swapped = pltpu.roll(kk, HEAD_DIM, 1)
    kat_ref[...] = jnp.where(is_ka, kk, swapped).T.astype(kat_ref.dtype)
    kit_ref[...] = jnp.where(is_ka, swapped, kk).T.astype(kit_ref.dtype)

    q_i = _dot(c_q, wqi_ref[...])
    qi_ref[...] = _rope(q_i, cos_w, sin_w, first_w).astype(qi_ref.dtype)

    q_a = _dot(c_q, wuq_ref[...])
    sq = q_a * q_a
    sq_hi = sq.astype(hmean_ref.dtype)
    sq_lo = (sq - sq_hi.astype(jnp.float32)).astype(hmean_ref.dtype)
    ms_q = _dot(sq_hi, hmean_ref[...]) + _dot(sq_lo, hmean_ref[...])
    q_a = q_a * lax.rsqrt(ms_q + EPS) * gqa_ref[...]
    qa_ref[...] = (_rope(q_a, cos_w, sin_w, first_w) * (HEAD_DIM ** -0.5)).astype(qa_ref.dtype)


def _proj(x1, pos, g_mix, w_a, w_w, w_b, g_cq, w_uq, w_qi, g_qa, g_ka, tm):
    n, d = x1.shape
    dt = _MXU_DT
    inv_freq = ROPE_THETA ** (-np.arange(0, HEAD_DIM, 2, dtype=np.float64) / HEAD_DIM)
    invf = jnp.asarray(np.tile(inv_freq, LANES // (HEAD_DIM // 2)).astype(np.float32)).reshape(1, LANES)
    head_of = np.arange(A_WIDTH) // HEAD_DIM
    hmean = jnp.asarray((head_of[:, None] == head_of[None, :]).astype(np.float32) / HEAD_DIM, dt)
    gqa = jnp.tile(g_qa, A_HEADS).reshape(1, A_WIDTH)
    gka = jnp.tile(g_ka, LANES // HEAD_DIM).reshape(1, LANES)

    def full(a):
        return pl.BlockSpec(a.shape, lambda i: (0,) * a.ndim)

    def rows(w):
        return pl.BlockSpec((tm, w), lambda i: (i, 0))

    args = (x1, pos, g_mix.reshape(1, d), w_a, w_w, w_b, g_cq.reshape(1, A_Q_RANK), w_uq, w_qi,
            gqa, gka, invf, hmean)
    in_specs = [rows(d), rows(1)] + [full(a) for a in args[2:]]

    def cols(w):
        return pl.BlockSpec((w, tm), lambda i: (0, i))

    def tok(w, t):
        return jax.ShapeDtypeStruct((n, w), t)

    def keymajor(w):
        return jax.ShapeDtypeStruct((w, n), dt)

    return pl.pallas_call(
        _proj_kernel,
        grid=(n // tm,),
        in_specs=in_specs,
        out_specs=[rows(A_WIDTH), rows(A_WIDTH), cols(LANES), cols(LANES), rows(A_V_LATENT), rows(LANES),
                   rows(2 * B_WIDTH), cols(B_WIDTH)],
        out_shape=[tok(A_WIDTH, dt), tok(A_WIDTH, dt), keymajor(LANES), keymajor(LANES), tok(A_V_LATENT, dt),
                   tok(LANES, jnp.float32), tok(2 * B_WIDTH, dt), keymajor(B_WIDTH)],
        compiler_params=_cparams(("parallel",)),
        name="proj",
    )(*args)


def _stack_heads(q_ref, out_ref, tq):
    lane = lax.broadcasted_iota(jnp.int32, (tq, LANES), 1)
    halves = (lane < HEAD_DIM, lane >= HEAD_DIM)
    for h in range(q_ref.shape[-1] // HEAD_DIM):
        pair = q_ref[0, :, (h // 2) * LANES:(h // 2 + 1) * LANES]
        out_ref[h * tq:(h + 1) * tq, :] = jnp.where(halves[h % 2], pair, jnp.zeros_like(pair))


def _dsa_kernel(qi_ref, w_ref, qa_ref, kit_ref, kat_ref, va_ref, wuv_ref, o_ref, sk_ref, qis_ref, qas_ref,
                acc_ref, *, tq, tkc, top_k):
    i = pl.program_id(1)
    n_chunks = (i * tq + tq + tkc - 1) // tkc
    t_row = i * tq + lax.broadcasted_iota(jnp.int32, (tq, 1), 0)

    def chunk(c):
        return pl.ds(pl.multiple_of(c * tkc, tkc), tkc)

    _stack_heads(qi_ref, qis_ref, tq)
    _stack_heads(qa_ref, qas_ref, tq)

    w = w_ref[0]

    def score_body(c, carry):
        d = _dot(qis_ref[...], kit_ref[:, chunk(c)])
        s = jnp.zeros((tq, tkc), jnp.float32)
        for h in range(IDX_HEADS):
            s = s + w[:, h:h + 1] * jnp.maximum(d[h * tq:(h + 1) * tq], 0.0)
        bits = lax.bitcast_convert_type(s, jnp.int32)
        key = bits ^ ((bits >> 31) & 0x7FFFFFFF)
        col = c * tkc + lax.broadcasted_iota(jnp.int32, (tq, tkc), 1)
        sk_ref[:, chunk(c)] = jnp.where(col <= t_row, key, INT_MIN)
        return carry

    lax.fori_loop(0, n_chunks, score_body, 0)

    def count_ge(thr):
        def body(c, part):
            m = (sk_ref[:, chunk(c)] >= thr).astype(jnp.int32)
            for j in range(tkc // LANES):
                part = part + m[:, j * LANES:(j + 1) * LANES]
            return part
        part = lax.fori_loop(0, n_chunks, body, jnp.zeros((tq, LANES), jnp.int32))
        return jnp.sum(part, axis=1, keepdims=True)

    def bit_body(b, t_u):
        cand = t_u | lax.shift_left(jnp.int32(1), 31 - b)
        cnt = count_ge(cand ^ INT_MIN)
        return jnp.where(cnt >= top_k, cand, t_u)

    t_u = lax.fori_loop(0, 32, bit_body, jnp.zeros((tq, 1), jnp.int32))
    thr = jnp.maximum(t_u ^ INT_MIN, INT_MIN + 1)

    acc_ref[...] = jnp.zeros_like(acc_ref)
    heads = range(A_HEADS)

    def attn_body(c, carry):
        m_run, l_run = carry
        lg = _dot(qas_ref[...], kat_ref[:, chunk(c)])
        bias = jnp.where(sk_ref[:, chunk(c)] >= thr, 0.0, NEG_BIG)
        v = va_ref[0, chunk(c), :]
        lhs = [lg[h * tq:(h + 1) * tq] + bias for h in heads]
        m_new = [jnp.maximum(m_run[h], jnp.max(lhs[h], axis=-1, keepdims=True)) for h in heads]
        ps = [jnp.exp(lhs[h] - m_new[h]) for h in heads]
        alphas = [jnp.exp(m_run[h] - m_new[h]) for h in heads]
        l_new = [alphas[h] * l_run[h] + jnp.sum(ps[h], axis=-1, keepdims=True) for h in heads]
        pv = _dot(jnp.concatenate([p.astype(v.dtype) for p in ps], axis=0), v)
        for h in heads:
            acc_ref[h] = alphas[h] * acc_ref[h] + pv[h * tq:(h + 1) * tq]
        return tuple(m_new), tuple(l_new)

    init = (tuple(jnp.full((tq, 1), NEG_BIG, jnp.float32) for _ in heads),
            tuple(jnp.zeros((tq, 1), jnp.float32) for _ in heads))
    _, l_fin = lax.fori_loop(0, n_chunks, attn_body, init)

    o_lats = [(acc_ref[h] / l_fin[h]).astype(wuv_ref.dtype) for h in heads]
    outs = [_dot(o_lats[h], wuv_ref[h]) for h in heads]
    o_ref[0] = jnp.concatenate(outs, axis=1).astype(o_ref.dtype)


def _dsa(qi, w_idx, qa, kit, kat, va, w_uv, top_k, tq, tkc):
    b, s, _ = qi.shape
    kern = functools.partial(_dsa_kernel, tq=tq, tkc=tkc, top_k=top_k)
    return pl.pallas_call(
        kern,
        grid=(b, s // tq),
        in_specs=[
            pl.BlockSpec((1, tq, A_WIDTH), lambda bi, i: (bi, i, 0)),
            pl.BlockSpec((1, tq, LANES), lambda bi, i: (bi, i, 0)),
            pl.BlockSpec((1, tq, A_WIDTH), lambda bi, i: (bi, i, 0)),
            pl.BlockSpec((LANES, s), lambda bi, i: (0, bi)),
            pl.BlockSpec((LANES, s), lambda bi, i: (0, bi)),
            pl.BlockSpec((1, s, A_V_LATENT), lambda bi, i: (bi, 0, 0)),
            pl.BlockSpec((A_HEADS, A_V_LATENT, HEAD_DIM), lambda bi, i: (0, 0, 0)),
        ],
        out_specs=pl.BlockSpec((1, tq, A_WIDTH), lambda bi, i: (bi, i, 0)),
        out_shape=jax.ShapeDtypeStruct((b, s, A_WIDTH), _MXU_DT),
        scratch_shapes=[pltpu.VMEM((tq, s), jnp.int32),
                        pltpu.VMEM((IDX_HEADS * tq, LANES), _MXU_DT),
                        pltpu.VMEM((A_HEADS * tq, LANES), _MXU_DT),
                        pltpu.VMEM((A_HEADS, tq, A_V_LATENT), jnp.float32)],
        compiler_params=_cparams(("parallel", "arbitrary")),
        name="dsa",
    )(qi, w_idx, qa, kit, kat, va, w_uv)


def _sb_kernel(q_ref, kt_ref, v_ref, o_ref, qs_ref, later_ref, acc_ref, *, tb):
    i = pl.program_id(1)
    row = lax.broadcasted_iota(jnp.int32, (tb, tb), 0)
    col = lax.broadcasted_iota(jnp.int32, (tb, tb), 1)
    neg_from = jnp.where(row >= col, -1.0, 0.0).astype(later_ref.dtype)
    later_ref[:tb, :] = neg_from
    later_ref[tb:, :] = neg_from
    strict = col < row
    _stack_heads(q_ref, qs_ref, tb)
    acc_ref[...] = jnp.zeros_like(acc_ref)

    def block(j, carries, diag):
        ks = pl.ds(pl.multiple_of(j * tb, tb), tb)
        later = later_ref[...]
        heads = range(B_HEADS)
        pairs = [slice((h // 2) * LANES, (h // 2 + 1) * LANES) for h in heads]
        zs = [_dot(qs_ref[h * tb:(h + 1) * tb, :], kt_ref[pairs[h], ks]) for h in heads]
        splits = []
        for h in heads:
            sp = jnp.maximum(zs[h], 0.0) + jnp.log(1.0 + jnp.exp2(jnp.abs(zs[h]) * (-LOG2E)))
            if diag:
                sp = jnp.where(strict, sp, 0.0)
            hi = sp.astype(later.dtype)
            lo = (sp - hi.astype(jnp.float32)).astype(later.dtype)
            splits.append(jnp.concatenate([hi, lo], axis=1))
        incls = [_dot(splits[h], later) for h in heads]
        probs, out = [], []
        for h in heads:
            a = jnp.exp2((zs[h] + incls[h] + carries[h]) * LOG2E)
            if diag:
                a = jnp.where(strict, a, 0.0)
            probs.append(a.astype(v_ref.dtype))
            out.append(carries[h] + incls[h][:, 0:1])
        for h in heads:
            acc_ref[h] += _dot(probs[h], v_ref[0, ks, pairs[h]])
        return tuple(out)

    carries = block(i, tuple(jnp.zeros((tb, 1), jnp.float32) for _ in range(B_HEADS)), True)
    lax.fori_loop(0, i, lambda k, c: block(i - 1 - k, c, False), carries)

    lane = lax.broadcasted_iota(jnp.int32, (tb, LANES), 1)
    for p in range(B_HEADS // 2):
        o_ref[0, :, p * LANES:(p + 1) * LANES] = jnp.where(
            lane < HEAD_DIM, acc_ref[2 * p], acc_ref[2 * p + 1]).astype(o_ref.dtype)


def _stickbreak(qv, kbt, tb):
    b, s, _ = qv.shape
    dt = qv.dtype
    return pl.pallas_call(
        functools.partial(_sb_kernel, tb=tb),
        grid=(b, s // tb),
        in_specs=[
            pl.BlockSpec((1, tb, B_WIDTH), lambda bi, i: (bi, i, 0)),
            pl.BlockSpec((B_WIDTH, s), lambda bi, i: (0, bi)),
            pl.BlockSpec((1, s, B_WIDTH), lambda bi, i: (bi, 0, 1)),
        ],
        out_specs=pl.BlockSpec((1, tb, B_WIDTH), lambda bi, i: (bi, i, 0)),
        out_shape=jax.ShapeDtypeStruct((b, s, B_WIDTH), dt),
        scratch_shapes=[pltpu.VMEM((B_HEADS * tb, LANES), dt),
                        pltpu.VMEM((2 * tb, tb), dt),
                        pltpu.VMEM((B_HEADS, tb, LANES), jnp.float32)],
        compiler_params=_cparams(("parallel", "arbitrary")),
        name="stickbreak",
    )(qv, kbt, qv)


def _merge_kernel(x_ref, ya_ref, yb_ref, gmix_ref, wg_ref, woa_ref, wob_ref, wout_ref, o_ref):
    x = x_ref[...]
    d = x.shape[-1]
    h = _rms(x, gmix_ref[...]).astype(wg_ref.dtype)
    gates = _dot(h, wg_ref[...])
    merged = (_sigmoid(gates[:, :d]) * _dot(ya_ref[...], woa_ref[...])
              + _sigmoid(gates[:, d:]) * _dot(yb_ref[...], wob_ref[...]))
    o_ref[...] = x + _dot(merged.astype(wout_ref.dtype), wout_ref[...])


def _merge(x1, ya, yb, g_mix, w_g, w_oa, w_ob, w_out, tm):
    n, d = x1.shape

    def full(a):
        return pl.BlockSpec(a.shape, lambda i: (0,) * a.ndim)

    def rows(w):
        return pl.BlockSpec((tm, w), lambda i: (i, 0))

    args = (x1, ya, yb, g_mix.reshape(1, d), w_g, w_oa, w_ob, w_out)
    return pl.pallas_call(
        _merge_kernel,
        grid=(n // tm,),
        in_specs=[rows(d), rows(A_WIDTH), rows(B_WIDTH)] + [full(a) for a in args[3:]],
        out_specs=rows(d),
        out_shape=jax.ShapeDtypeStruct((n, d), jnp.float32),
        compiler_params=_cparams(("parallel",)),
        name="merge",
    )(*args)


def _pick(n, prefs):
    for p in prefs:
        if n % p == 0:
            return p
    return n


def _layer(x, positions, g_ffn1, w1_gate, w1_up, w1_down, g_mix, w_in, g_cq, w_uq_a, w_q_idx,
           g_q_a, g_k_a, w_uv_a, w_o_a, w_o_b, w_out, g_ffn2, w2_gate, w2_up, w2_down):
    bsz, seq, d = x.shape
    n = bsz * seq
    dt = _MXU_DT
    top_k = min(MAX_TOPK, seq // 4)
    tm_ffn = _pick(n, (1024, 512, 256))
    tf = _pick(w1_gate.shape[1], (256, 128))
    tm = _pick(n, (512, 256))
    tq = 128
    tkc = _pick(seq, (512, 256, 128))
    tb = _pick(seq, (256, 128))

    xf = x.reshape(n, d)
    x1 = _ffn(xf, g_ffn1, w1_gate.astype(dt), w1_up.astype(dt), w1_down.astype(dt), tm_ffn, tf)

    o = np.cumsum((A_Q_RANK, HEAD_DIM, A_V_LATENT, IDX_DIM, IDX_HEADS, B_WIDTH, B_WIDTH, B_WIDTH, d, d))
    w_a = jnp.concatenate([w_in[:, :o[0]], w_in[:, o[0]:o[1]], w_in[:, o[2]:o[3]], w_in[:, o[1]:o[2]]],
                          axis=1).astype(dt)
    w_w = jnp.pad(w_in[:, o[3]:o[4]], ((0, 0), (0, LANES - IDX_HEADS))).astype(dt)
    w_b = jnp.concatenate([w_in[:, o[4]:o[5]] * (HEAD_DIM ** -0.5), w_in[:, o[5]:o[7]]], axis=1).astype(dt)
    w_g = w_in[:, o[7]:].astype(dt)

    qa, qi, kat, kit, va, widx, qv, kbt = _proj(
        x1, positions.reshape(n, 1), g_mix, w_a, w_w, w_b, g_cq, w_uq_a.astype(dt), w_q_idx.astype(dt),
        g_q_a, g_k_a, tm)

    ya = _dsa(qi.reshape(bsz, seq, A_WIDTH), widx.reshape(bsz, seq, LANES), qa.reshape(bsz, seq, A_WIDTH),
              kit, kat, va.reshape(bsz, seq, A_V_LATENT), w_uv_a.astype(dt), top_k, tq, tkc)
    yb = _stickbreak(qv.reshape(bsz, seq, 2 * B_WIDTH), kbt, tb)

    x2 = _merge(x1, ya.reshape(n, A_WIDTH), yb.reshape(n, B_WIDTH), g_mix, w_g, w_o_a.astype(dt),
                w_o_b.astype(dt), w_out.astype(dt), tm)
    x3 = _ffn(x2, g_ffn2, w2_gate.astype(dt), w2_up.astype(dt), w2_down.astype(dt), tm_ffn, tf)
    return x3.reshape(bsz, seq, d)


def kernel(x, positions, g_ffn1, w1_gate, w1_up, w1_down, g_mix, w_in, g_cq, w_uq_a, w_q_idx, g_q_a, g_k_a,
           w_uv_a, w_o_a, w_o_b, w_out, g_ffn2, w2_gate, w2_up, w2_down):
    depth = g_ffn1.shape[0]
    for l in range(depth):
        x = _layer(x, positions, g_ffn1[l], w1_gate[l], w1_up[l], w1_down[l], g_mix[l], w_in[l], g_cq[l],
                   w_uq_a[l], w_q_idx[l], g_q_a[l], g_k_a[l], w_uv_a[l], w_o_a[l], w_o_b[l], w_out[l],
                   g_ffn2[l], w2_gate[l], w2_up[l], w2_down[l])
    return x
```

```python
import functools

import numpy as np
import jax
import jax.numpy as jnp
from jax import lax
from jax.experimental import pallas as pl
from jax.experimental.pallas import tpu as pltpu

HEAD_DIM = 64
A_HEADS = 8
A_Q_RANK = 256
A_V_LATENT = 128
IDX_HEADS = 8
IDX_DIM = 64
B_HEADS = 8
MAX_TOPK = 256
ROPE_THETA = 10000.0
EPS = 1e-6
A_WIDTH = A_HEADS * HEAD_DIM
B_WIDTH = B_HEADS * HEAD_DIM
LANES = 128
SUBLANES = 8
INT_MIN = -(2 ** 31)
NEG_BIG = -1e30
LOG2E = 1.4426950408889634
SEARCH_CHUNK = 512

_MXU_DT = jnp.bfloat16
_VMEM_LIMIT = 56 * 1024 * 1024


def _cparams(sem):
    return pltpu.CompilerParams(dimension_semantics=sem, vmem_limit_bytes=_VMEM_LIMIT)


def _dot(a, b):
    return jnp.dot(a, b, preferred_element_type=jnp.float32)


def _sigmoid(x):
    return 1.0 / (1.0 + jnp.exp(-x))


def _rms(x, g):
    ms = jnp.mean(x * x, axis=-1, keepdims=True)
    return x * lax.rsqrt(ms + EPS) * g


def _ffn_kernel(x_ref, g_ref, wg_ref, wu_ref, wd_ref, o_ref, h_ref, acc_ref):
    f = pl.program_id(1)

    @pl.when(f == 0)
    def _():
        h_ref[...] = _rms(x_ref[...], g_ref[...]).astype(h_ref.dtype)
        acc_ref[...] = jnp.zeros_like(acc_ref)

    h = h_ref[...]
    gate = _dot(h, wg_ref[...])
    up = _dot(h, wu_ref[...])
    act = (gate * _sigmoid(gate) * up).astype(wd_ref.dtype)
    acc_ref[...] += _dot(act, wd_ref[...])

    @pl.when(f == pl.num_programs(1) - 1)
    def _():
        o_ref[...] = x_ref[...] + 0.5 * acc_ref[...]


def _ffn(x, g, wg, wu, wd, tm, tf):
    n, d = x.shape
    dff = wg.shape[1]
    return pl.pallas_call(
        _ffn_kernel,
        grid=(n // tm, dff // tf),
        in_specs=[
            pl.BlockSpec((tm, d), lambda i, f: (i, 0)),
            pl.BlockSpec((1, d), lambda i, f: (0, 0)),
            pl.BlockSpec((d, tf), lambda i, f: (0, f)),
            pl.BlockSpec((d, tf), lambda i, f: (0, f)),
            pl.BlockSpec((tf, d), lambda i, f: (f, 0)),
        ],
        out_specs=pl.BlockSpec((tm, d), lambda i, f: (i, 0)),
        out_shape=jax.ShapeDtypeStruct((n, d), jnp.float32),
        scratch_shapes=[pltpu.VMEM((tm, d), _MXU_DT), pltpu.VMEM((tm, d), jnp.float32)],
        compiler_params=_cparams(("parallel", "arbitrary")),
        name="ffn",
    )(x, g.reshape(1, d), wg, wu, wd)


def _rope(x, cos, sin_signed, first_half):
    n = x.shape[-1]
    half = HEAD_DIM // 2
    fwd = pltpu.roll(x, n - half, 1)
    bwd = pltpu.roll(x, half, 1)
    return x * cos + jnp.where(first_half, fwd, bwd) * sin_signed


def _proj_kernel(x_ref, pos_ref, gmix_ref, wa_ref, ww_ref, wb_ref, gcq_ref, wuq_ref, wqi_ref,
                 gqa_ref, gka_ref, invf_ref, hmean_ref,
                 qat_ref, qit_ref, ka_ref, ki_ref, vat_ref, wit_ref, qv_ref, kbt_ref):
    h = _rms(x_ref[...], gmix_ref[...]).astype(wa_ref.dtype)

    r = _dot(h, wb_ref[...])
    qv_ref[:, :B_WIDTH] = r[:, :B_WIDTH].astype(qv_ref.dtype)
    qv_ref[:, B_WIDTH:] = r[:, 2 * B_WIDTH:].astype(qv_ref.dtype)
    kbt_ref[...] = r[:, B_WIDTH:2 * B_WIDTH].T.astype(kbt_ref.dtype)
    wit_ref[...] = (_dot(h, ww_ref[...]) * ((IDX_HEADS ** -0.5) * (IDX_DIM ** -0.5))).T

    pa = _dot(h, wa_ref[...])
    c_q = _rms(pa[:, :A_Q_RANK], gcq_ref[...]).astype(wuq_ref.dtype)
    kk = pa[:, A_Q_RANK:A_Q_RANK + LANES]
    vat_ref[...] = pa[:, A_Q_RANK + LANES:].T.astype(vat_ref.dtype)

    ang = pos_ref[...].astype(jnp.float32) * invf_ref[...]
    cos = jnp.cos(ang)
    sin = jnp.sin(ang)
    lane = lax.broadcasted_iota(jnp.int32, cos.shape, 1)
    first_half = (lane & (HEAD_DIM // 2)) == 0
    sin_signed = jnp.where(first_half, -sin, sin)
    reps = A_WIDTH // LANES
    cos_w = jnp.concatenate([cos] * reps, axis=1)
    sin_w = jnp.concatenate([sin_signed] * reps, axis=1)
    first_w = (lax.broadcasted_iota(jnp.int32, cos_w.shape, 1) & (HEAD_DIM // 2)) == 0

    is_ka = lane < HEAD_DIM
    ms_k = jnp.sum(jnp.where(is_ka, kk * kk, 0.0), axis=-1, keepdims=True) * (1.0 / HEAD_DIM)
    kscale = jnp.where(is_ka, lax.rsqrt(ms_k + EPS) * gka_ref[...], 1.0)
    kk = _rope(kk * kscale, cos, sin_signed, first_half)
    swapped = pltpu.roll(kk, HEAD_DIM, 1)
    ka_ref[...] = jnp.where(is_ka, kk, swapped).astype(ka_ref.dtype)
    ki_ref[...] = jnp.where(is_ka, swapped, kk).astype(ki_ref.dtype)

    q_i = _dot(c_q, wqi_ref[...])
    qit_ref[...] = _rope(q_i, cos_w, sin_w, first_w).T.astype(qit_ref.dtype)

    q_a = _dot(c_q, wuq_ref[...])
    sq = q_a * q_a
    sq_hi = sq.astype(hmean_ref.dtype)
    sq_lo = (sq - sq_hi.astype(jnp.float32)).astype(hmean_ref.dtype)
    ms_q = _dot(sq_hi, hmean_ref[...]) + _dot(sq_lo, hmean_ref[...])
    q_a = q_a * lax.rsqrt(ms_q + EPS) * gqa_ref[...]
    qat_ref[...] = (_rope(q_a, cos_w, sin_w, first_w) * (HEAD_DIM ** -0.5)).T.astype(qat_ref.dtype)


def _proj(x1, pos, g_mix, w_a, w_w, w_b, g_cq, w_uq, w_qi, g_qa, g_ka, tm):
    n, d = x1.shape
    dt = _MXU_DT
    inv_freq = ROPE_THETA ** (-np.arange(0, HEAD_DIM, 2, dtype=np.float64) / HEAD_DIM)
    invf = jnp.asarray(np.tile(inv_freq, LANES // (HEAD_DIM // 2)).astype(np.float32)).reshape(1, LANES)
    head_of = np.arange(A_WIDTH) // HEAD_DIM
    hmean = jnp.asarray((head_of[:, None] == head_of[None, :]).astype(np.float32) / HEAD_DIM, dt)
    gqa = jnp.tile(g_qa, A_HEADS).reshape(1, A_WIDTH)
    gka = jnp.tile(g_ka, LANES // HEAD_DIM).reshape(1, LANES)

    def full(a):
        return pl.BlockSpec(a.shape, lambda i: (0,) * a.ndim)

    def rows(w):
        return pl.BlockSpec((tm, w), lambda i: (i, 0))

    args = (x1, pos, g_mix.reshape(1, d), w_a, w_w, w_b, g_cq.reshape(1, A_Q_RANK), w_uq, w_qi,
            gqa, gka, invf, hmean)
    in_specs = [rows(d), rows(1)] + [full(a) for a in args[2:]]

    def cols(w):
        return pl.BlockSpec((w, tm), lambda i: (0, i))

    def tok(w, t):
        return jax.ShapeDtypeStruct((n, w), t)

    def keymajor(w, t=dt):
        return jax.ShapeDtypeStruct((w, n), t)

    return pl.pallas_call(
        _proj_kernel,
        grid=(n // tm,),
        in_specs=in_specs,
        out_specs=[cols(A_WIDTH), cols(A_WIDTH), rows(LANES), rows(LANES), cols(A_V_LATENT), cols(LANES),
                   rows(2 * B_WIDTH), cols(B_WIDTH)],
        out_shape=[keymajor(A_WIDTH), keymajor(A_WIDTH), tok(LANES, dt), tok(LANES, dt), keymajor(A_V_LATENT),
                   keymajor(LANES, jnp.float32), tok(2 * B_WIDTH, dt), keymajor(B_WIDTH)],
        compiler_params=_cparams(("parallel",)),
        name="proj",
    )(*args)


def _stack_heads_t(qt_ref, out_ref, tq):
    row = lax.broadcasted_iota(jnp.int32, (LANES, tq), 0)
    halves = (row < HEAD_DIM, row >= HEAD_DIM)
    for h in range(qt_ref.shape[0] // HEAD_DIM):
        pair = qt_ref[(h // 2) * LANES:(h // 2 + 1) * LANES, :]
        out_ref[:, h * tq:(h + 1) * tq] = jnp.where(halves[h % 2], pair, jnp.zeros_like(pair))


def _dsa_kernel(qit_ref, wt_ref, qat_ref, ki_ref, ka_ref, vt_ref, wuvt_ref, o_ref,
                sk_ref, qis_ref, qas_ref, acc_ref, *, tq, top_k):
    i = pl.program_id(1)
    n_chunks = i + 1
    n_search = (n_chunks * tq + SEARCH_CHUNK - 1) // SEARCH_CHUNK
    t_col = i * tq + lax.broadcasted_iota(jnp.int32, (1, tq), 1)
    nh = A_HEADS

    def keys(c, width):
        return pl.ds(pl.multiple_of(c * width, width), width)

    _stack_heads_t(qit_ref, qis_ref, tq)
    _stack_heads_t(qat_ref, qas_ref, tq)

    def score_body(c, carry):
        d = _dot(ki_ref[keys(c, tq), :], qis_ref[...])
        s = jnp.zeros((tq, tq), jnp.float32)
        for h in range(IDX_HEADS):
            s = s + wt_ref[h:h + 1, :] * jnp.maximum(d[:, h * tq:(h + 1) * tq], 0.0)
        bits = lax.bitcast_convert_type(s, jnp.int32)
        key = bits ^ ((bits >> 31) & 0x7FFFFFFF)
        s_pos = c * tq + lax.broadcasted_iota(jnp.int32, (tq, 1), 0)
        sk_ref[keys(c, tq), :] = jnp.where(s_pos <= t_col, key, INT_MIN)
        return carry

    lax.fori_loop(0, n_chunks, score_body, 0)

    def fill_body(c, carry):
        sk_ref[keys(c, tq), :] = jnp.full((tq, tq), INT_MIN, jnp.int32)
        return carry

    lax.fori_loop(n_chunks, n_search * (SEARCH_CHUNK // tq), fill_body, 0)

    def count_ge(thr):
        def body(c, part):
            ge = jnp.where(sk_ref[keys(c, SEARCH_CHUNK), :] >= thr, 1.0, 0.0)
            ge = ge.reshape(SUBLANES, SEARCH_CHUNK // (SUBLANES * SUBLANES), SUBLANES, tq)
            return part + jnp.sum(jnp.sum(ge, axis=1), axis=0)
        part = lax.fori_loop(0, n_search, body, jnp.zeros((SUBLANES, tq), jnp.float32))
        return jnp.sum(part, axis=0, keepdims=True)

    def bit_body(b, t_u):
        cand = t_u | lax.shift_left(jnp.int32(1), 31 - b)
        return jnp.where(count_ge(cand ^ INT_MIN) >= top_k, cand, t_u)

    t_u = lax.fori_loop(0, 32, bit_body, jnp.zeros((1, tq), jnp.int32))
    thr = jnp.maximum(t_u ^ INT_MIN, INT_MIN + 1)

    excess = count_ge(thr) - top_k

    @pl.when(jnp.max(excess) > 0.0)
    def _():
        def ties_before(limit):
            def body(c, part):
                s_pos = c * SEARCH_CHUNK + lax.broadcasted_iota(jnp.int32, (SEARCH_CHUNK, 1), 0)
                hit = jnp.where(sk_ref[keys(c, SEARCH_CHUNK), :] == thr, 1.0, 0.0)
                hit = jnp.where(s_pos < limit, hit, 0.0)
                hit = hit.reshape(SUBLANES, SEARCH_CHUNK // (SUBLANES * SUBLANES), SUBLANES, tq)
                return part + jnp.sum(jnp.sum(hit, axis=1), axis=0)
            part = lax.fori_loop(0, n_search, body, jnp.zeros((SUBLANES, tq), jnp.float32))
            return jnp.sum(part, axis=0, keepdims=True)

        n_ties = ties_before(jnp.full((1, tq), n_search * SEARCH_CHUNK, jnp.int32))
        keep = n_ties - excess
        pos_bits = sk_ref.shape[0].bit_length()

        def pos_body(b, limit):
            cand = limit | lax.shift_left(jnp.int32(1), pos_bits - 1 - b)
            return jnp.where(ties_before(cand) <= keep, cand, limit)

        limit = lax.fori_loop(0, pos_bits, pos_body, jnp.zeros((1, tq), jnp.int32))

        def demote_body(c, carry):
            s_pos = c * SEARCH_CHUNK + lax.broadcasted_iota(jnp.int32, (SEARCH_CHUNK, 1), 0)
            key = sk_ref[keys(c, SEARCH_CHUNK), :]
            sk_ref[keys(c, SEARCH_CHUNK), :] = jnp.where(key == thr, jnp.where(s_pos >= limit, INT_MIN, key), key)
            return carry

        lax.fori_loop(0, n_search, demote_body, 0)

    acc_ref[...] = jnp.zeros_like(acc_ref)

    def attn_body(c, carry):
        m_run, l_run = carry
        lg = _dot(ka_ref[keys(c, tq), :], qas_ref[...])
        bias = jnp.where(sk_ref[keys(c, tq), :] >= thr, 0.0, NEG_BIG)
        x = lg + jnp.concatenate([bias] * nh, axis=1)
        m_new = jnp.maximum(m_run, jnp.max(x, axis=0, keepdims=True))
        alpha = jnp.exp(m_run - m_new)
        p = jnp.exp(x - m_new)
        l_new = alpha * l_run + jnp.sum(p, axis=0, keepdims=True)
        acc_ref[...] = alpha * acc_ref[...] + _dot(vt_ref[:, keys(c, tq)], p.astype(vt_ref.dtype))
        return m_new, l_new

    init = (jnp.full((1, nh * tq), NEG_BIG, jnp.float32), jnp.zeros((1, nh * tq), jnp.float32))
    _, l_fin = lax.fori_loop(0, n_chunks, attn_body, init)

    o_lat = (acc_ref[...] / l_fin).astype(wuvt_ref.dtype)
    outs = [_dot(wuvt_ref[h], o_lat[:, h * tq:(h + 1) * tq]) for h in range(nh)]
    o_ref[0] = jnp.concatenate(outs, axis=0).T.astype(o_ref.dtype)


def _dsa(qit, wit, qat, ki, ka, vat, w_uv_t, bsz, top_k, tq):
    n = qit.shape[1]
    s = n // bsz
    nq = s // tq
    assert s % SEARCH_CHUNK == 0 and SEARCH_CHUNK % tq == 0
    dt = qit.dtype

    def qcols(w):
        return pl.BlockSpec((w, tq), lambda bi, i: (0, bi * nq + i))

    return pl.pallas_call(
        functools.partial(_dsa_kernel, tq=tq, top_k=top_k),
        grid=(bsz, nq),
        in_specs=[
            qcols(A_WIDTH), qcols(LANES), qcols(A_WIDTH),
            pl.BlockSpec((s, LANES), lambda bi, i: (bi, 0)),
            pl.BlockSpec((s, LANES), lambda bi, i: (bi, 0)),
            pl.BlockSpec((A_V_LATENT, s), lambda bi, i: (0, bi)),
            pl.BlockSpec((A_HEADS, HEAD_DIM, A_V_LATENT), lambda bi, i: (0, 0, 0)),
        ],
        out_specs=pl.BlockSpec((1, tq, A_WIDTH), lambda bi, i: (bi, i, 0)),
        out_shape=jax.ShapeDtypeStruct((bsz, s, A_WIDTH), dt),
        scratch_shapes=[pltpu.VMEM((s, tq), jnp.int32),
                        pltpu.VMEM((LANES, IDX_HEADS * tq), dt),
                        pltpu.VMEM((LANES, A_HEADS * tq), dt),
                        pltpu.VMEM((A_V_LATENT, A_HEADS * tq), jnp.float32)],
        compiler_params=_cparams(("parallel", "arbitrary")),
        name="dsa",
    )(qit, wit, qat, ki, ka, vat, w_uv_t)


def _stack_heads(q_ref, out_ref, tq):
    lane = lax.broadcasted_iota(jnp.int32, (tq, LANES), 1)
    halves = (lane < HEAD_DIM, lane >= HEAD_DIM)
    for h in range(q_ref.shape[-1] // HEAD_DIM):
        pair = q_ref[0, :, (h // 2) * LANES:(h // 2 + 1) * LANES]
        out_ref[h * tq:(h + 1) * tq, :] = jnp.where(halves[h % 2], pair, jnp.zeros_like(pair))


def _sb_kernel(q_ref, kt_ref, v_ref, o_ref, qs_ref, later_ref, acc_ref, *, tb):
    i = pl.program_id(1)
    row = lax.broadcasted_iota(jnp.int32, (tb, tb), 0)
    col = lax.broadcasted_iota(jnp.int32, (tb, tb), 1)
    neg_from = jnp.where(row >= col, -1.0, 0.0).astype(later_ref.dtype)
    later_ref[:tb, :] = neg_from
    later_ref[tb:, :] = neg_from
    strict = col < row
    _stack_heads(q_ref, qs_ref, tb)
    acc_ref[...] = jnp.zeros_like(acc_ref)

    def block(j, carries, diag):
        ks = pl.ds(pl.multiple_of(j * tb, tb), tb)
        later = later_ref[...]
        heads = range(B_HEADS)
        pairs = [slice((h // 2) * LANES, (h // 2 + 1) * LANES) for h in heads]
        zs = [_dot(qs_ref[h * tb:(h + 1) * tb, :], kt_ref[pairs[h], ks]) for h in heads]
        splits = []
        for h in heads:
            sp = jnp.maximum(zs[h], 0.0) + jnp.log(1.0 + jnp.exp2(jnp.abs(zs[h]) * (-LOG2E)))
            if diag:
                sp = jnp.where(strict, sp, 0.0)
            hi = sp.astype(later.dtype)
            lo = (sp - hi.astype(jnp.float32)).astype(later.dtype)
            splits.append(jnp.concatenate([hi, lo], axis=1))
        incls = [_dot(splits[h], later) for h in heads]
        probs, out = [], []
        for h in heads:
            a = jnp.exp2((zs[h] + incls[h] + carries[h]) * LOG2E)
            if diag:
                a = jnp.where(strict, a, 0.0)
            probs.append(a.astype(v_ref.dtype))
            out.append(carries[h] + incls[h][:, 0:1])
        for h in heads:
            acc_ref[h] += _dot(probs[h], v_ref[0, ks, pairs[h]])
        return tuple(out)

    carries = block(i, tuple(jnp.zeros((tb, 1), jnp.float32) for _ in range(B_HEADS)), True)
    lax.fori_loop(0, i, lambda k, c: block(i - 1 - k, c, False), carries)

    lane = lax.broadcasted_iota(jnp.int32, (tb, LANES), 1)
    for p in range(B_HEADS // 2):
        o_ref[0, :, p * LANES:(p + 1) * LANES] = jnp.where(
            lane < HEAD_DIM, acc_ref[2 * p], acc_ref[2 * p + 1]).astype(o_ref.dtype)


def _stickbreak(qv, kbt, tb):
    b, s, _ = qv.shape
    dt = qv.dtype
    return pl.pallas_call(
        functools.partial(_sb_kernel, tb=tb),
        grid=(b, s // tb),
        in_specs=[
            pl.BlockSpec((1, tb, B_WIDTH), lambda bi, i: (bi, i, 0)),
            pl.BlockSpec((B_WIDTH, s), lambda bi, i: (0, bi)),
            pl.BlockSpec((1, s, B_WIDTH), lambda bi, i: (bi, 0, 1)),
        ],
        out_specs=pl.BlockSpec((1, tb, B_WIDTH), lambda bi, i: (bi, i, 0)),
        out_shape=jax.ShapeDtypeStruct((b, s, B_WIDTH), dt),
        scratch_shapes=[pltpu.VMEM((B_HEADS * tb, LANES), dt),
                        pltpu.VMEM((2 * tb, tb), dt),
                        pltpu.VMEM((B_HEADS, tb, LANES), jnp.float32)],
        compiler_params=_cparams(("parallel", "arbitrary")),
        name="stickbreak",
    )(qv, kbt, qv)


def _merge_kernel(x_ref, ya_ref, yb_ref, gmix_ref, wg_ref, woa_ref, wob_ref, wout_ref, o_ref):
    x = x_ref[...]
    d = x.shape[-1]
    h = _rms(x, gmix_ref[...]).astype(wg_ref.dtype)
    gates = _dot(h, wg_ref[...])
    merged = (_sigmoid(gates[:, :d]) * _dot(ya_ref[...], woa_ref[...])
              + _sigmoid(gates[:, d:]) * _dot(yb_ref[...], wob_ref[...]))
    o_ref[...] = x + _dot(merged.astype(wout_ref.dtype), wout_ref[...])


def _merge(x1, ya, yb, g_mix, w_g, w_oa, w_ob, w_out, tm):
    n, d = x1.shape

    def full(a):
        return pl.BlockSpec(a.shape, lambda i: (0,) * a.ndim)

    def rows(w):
        return pl.BlockSpec((tm, w), lambda i: (i, 0))

    args = (x1, ya, yb, g_mix.reshape(1, d), w_g, w_oa, w_ob, w_out)
    return pl.pallas_call(
        _merge_kernel,
        grid=(n // tm,),
        in_specs=[rows(d), rows(A_WIDTH), rows(B_WIDTH)] + [full(a) for a in args[3:]],
        out_specs=rows(d),
        out_shape=jax.ShapeDtypeStruct((n, d), jnp.float32),
        compiler_params=_cparams(("parallel",)),
        name="merge",
    )(*args)


def _pick(n, prefs):
    for p in prefs:
        if n % p == 0:
            return p
    return n


def _layer(x, positions, g_ffn1, w1_gate, w1_up, w1_down, g_mix, w_in, g_cq, w_uq_a, w_q_idx,
           g_q_a, g_k_a, w_uv_a, w_o_a, w_o_b, w_out, g_ffn2, w2_gate, w2_up, w2_down):
    bsz, seq, d = x.shape
    n = bsz * seq
    dt = _MXU_DT
    top_k = min(MAX_TOPK, seq // 4)
    tm_ffn = _pick(n, (1024, 512, 256))
    tf = _pick(w1_gate.shape[1], (256, 128))
    tm = _pick(n, (512, 256))
    tq = _pick(seq, (256, 128))
    tb = _pick(seq, (256, 128))

    xf = x.reshape(n, d)
    x1 = _ffn(xf, g_ffn1, w1_gate.astype(dt), w1_up.astype(dt), w1_down.astype(dt), tm_ffn, tf)

    o = np.cumsum((A_Q_RANK, HEAD_DIM, A_V_LATENT, IDX_DIM, IDX_HEADS, B_WIDTH, B_WIDTH, B_WIDTH, d, d))
    w_a = jnp.concatenate([w_in[:, :o[0]], w_in[:, o[0]:o[1]], w_in[:, o[2]:o[3]], w_in[:, o[1]:o[2]]],
                          axis=1).astype(dt)
    w_w = jnp.pad(w_in[:, o[3]:o[4]], ((0, 0), (0, LANES - IDX_HEADS))).astype(dt)
    w_b = jnp.concatenate([w_in[:, o[4]:o[5]] * (HEAD_DIM ** -0.5), w_in[:, o[5]:o[7]]], axis=1).astype(dt)
    w_g = w_in[:, o[7]:].astype(dt)

    qat, qit, ka, ki, vat, wit, qv, kbt = _proj(
        x1, positions.reshape(n, 1), g_mix, w_a, w_w, w_b, g_cq, w_uq_a.astype(dt), w_q_idx.astype(dt),
        g_q_a, g_k_a, tm)

    ya = _dsa(qit, wit, qat, ki, ka, vat, w_uv_a.transpose(0, 2, 1).astype(dt), bsz, top_k, tq)
    yb = _stickbreak(qv.reshape(bsz, seq, 2 * B_WIDTH), kbt, tb)

    x2 = _merge(x1, ya.reshape(n, A_WIDTH), yb.reshape(n, B_WIDTH), g_mix, w_g, w_o_a.astype(dt),
                w_o_b.astype(dt), w_out.astype(dt), tm)
    x3 = _ffn(x2, g_ffn2, w2_gate.astype(dt), w2_up.astype(dt), w2_down.astype(dt), tm_ffn, tf)
    return x3.reshape(bsz, seq, d)


def kernel(x, positions, g_ffn1, w1_gate, w1_up, w1_down, g_mix, w_in, g_cq, w_uq_a, w_q_idx, g_q_a, g_k_a,
           w_uv_a, w_o_a, w_o_b, w_out, g_ffn2, w2_gate, w2_up, w2_down):
    depth = g_ffn1.shape[0]
    for l in range(depth):
        x = _layer(x, positions, g_ffn1[l], w1_gate[l], w1_up[l], w1_down[l], g_mix[l], w_in[l], g_cq[l],
                   w_uq_a[l], w_q_idx[l], g_q_a[l], g_k_a[l], w_uv_a[l], w_o_a[l], w_o_b[l], w_out[l],
                   g_ffn2[l], w2_gate[l], w2_up[l], w2_down[l])
    return x
```

```python
import functools

import numpy as np
import jax
import jax.numpy as jnp
from jax import lax
from jax.experimental import pallas as pl
from jax.experimental.pallas import tpu as pltpu

HEAD_DIM = 64
A_HEADS = 8
A_Q_RANK = 256
A_V_LATENT = 128
IDX_HEADS = 8
IDX_DIM = 64
B_HEADS = 8
MAX_TOPK = 256
ROPE_THETA = 10000.0
EPS = 1e-6
A_WIDTH = A_HEADS * HEAD_DIM
B_WIDTH = B_HEADS * HEAD_DIM
LANES = 128
SUBLANES = 8
INT_MIN = -(2 ** 31)
NEG_BIG = -1e30
LOG2E = 1.4426950408889634
SEARCH_CHUNK = 512

_MXU_DT = jnp.bfloat16
_VMEM_LIMIT = 56 * 1024 * 1024
TOP16 = -65536
MIN_NORMAL_BITS = 0x00800000
MIN_NORMAL = float(np.float32(2.0) ** -126)


def _cparams(sem):
    return pltpu.CompilerParams(dimension_semantics=sem, vmem_limit_bytes=_VMEM_LIMIT)


def _dot(a, b):
    return jnp.dot(a, b, preferred_element_type=jnp.float32)


def _sigmoid(x):
    return 1.0 / (1.0 + jnp.exp(-x))


def _rms(x, g):
    ms = jnp.mean(x * x, axis=-1, keepdims=True)
    return x * lax.rsqrt(ms + EPS) * g


def _ffn_kernel(x_ref, g_ref, wg_ref, wu_ref, wd_ref, o_ref, h_ref):
    @pl.when(pl.program_id(1) == 0)
    def _():
        x = x_ref[...]
        h_ref[...] = _rms(x, g_ref[...]).astype(h_ref.dtype)
        o_ref[...] = x

    h = h_ref[...]
    gate = _dot(h, wg_ref[...])
    up = _dot(h, wu_ref[...])
    act = (gate * _sigmoid(gate) * up).astype(wd_ref.dtype)
    o_ref[...] += _dot(act, wd_ref[...])


def _ffn(x, g, wg, wu, wd, tm, tf):
    n, d = x.shape
    dff = wg.shape[1]
    return pl.pallas_call(
        _ffn_kernel,
        grid=(n // tm, dff // tf),
        in_specs=[
            pl.BlockSpec((tm, d), lambda i, f: (i, 0)),
            pl.BlockSpec((1, d), lambda i, f: (0, 0)),
            pl.BlockSpec((d, tf), lambda i, f: (0, f)),
            pl.BlockSpec((d, tf), lambda i, f: (0, f)),
            pl.BlockSpec((tf, d), lambda i, f: (f, 0)),
        ],
        out_specs=pl.BlockSpec((tm, d), lambda i, f: (i, 0)),
        out_shape=jax.ShapeDtypeStruct((n, d), jnp.float32),
        scratch_shapes=[pltpu.VMEM((tm, d), _MXU_DT)],
        compiler_params=_cparams(("parallel", "arbitrary")),
        name="ffn",
    )(x, g.reshape(1, d), wg, wu, wd)


def _rope(x, cos, sin_signed, first_half):
    n = x.shape[-1]
    half = HEAD_DIM // 2
    fwd = pltpu.roll(x, n - half, 1)
    bwd = pltpu.roll(x, half, 1)
    return x * cos + jnp.where(first_half, fwd, bwd) * sin_signed


def _proj_kernel(x_ref, pos_ref, gmix_ref, wa_ref, ww_ref, wb_ref, gcq_ref, wuq_ref, wqi_ref,
                 gqa_ref, gka_ref, invf_ref, hmean_ref,
                 qat_ref, qit_ref, ka_ref, ki_ref, vat_ref, wit_ref, qv_ref, kbt_ref):
    h = _rms(x_ref[...], gmix_ref[...]).astype(wa_ref.dtype)

    r = _dot(h, wb_ref[...])
    qv_ref[:, :B_WIDTH] = r[:, :B_WIDTH].astype(qv_ref.dtype)
    qv_ref[:, B_WIDTH:] = r[:, 2 * B_WIDTH:].astype(qv_ref.dtype)
    kbt_ref[...] = r[:, B_WIDTH:2 * B_WIDTH].T.astype(kbt_ref.dtype)
    wit_ref[...] = (_dot(h, ww_ref[...]) * ((IDX_HEADS ** -0.5) * (IDX_DIM ** -0.5))).T

    pa = _dot(h, wa_ref[...])
    c_q = _rms(pa[:, :A_Q_RANK], gcq_ref[...]).astype(wuq_ref.dtype)
    kk = pa[:, A_Q_RANK:A_Q_RANK + LANES]
    vat_ref[...] = pa[:, A_Q_RANK + LANES:].T.astype(vat_ref.dtype)

    ang = pos_ref[...].astype(jnp.float32) * invf_ref[...]
    cos = jnp.cos(ang)
    sin = jnp.sin(ang)
    lane = lax.broadcasted_iota(jnp.int32, cos.shape, 1)
    first_half = (lane & (HEAD_DIM // 2)) == 0
    sin_signed = jnp.where(first_half, -sin, sin)
    reps = A_WIDTH // LANES
    cos_w = jnp.concatenate([cos] * reps, axis=1)
    sin_w = jnp.concatenate([sin_signed] * reps, axis=1)
    first_w = (lax.broadcasted_iota(jnp.int32, cos_w.shape, 1) & (HEAD_DIM // 2)) == 0

    is_ka = lane < HEAD_DIM
    ms_k = jnp.sum(jnp.where(is_ka, kk * kk, 0.0), axis=-1, keepdims=True) * (1.0 / HEAD_DIM)
    kscale = jnp.where(is_ka, lax.rsqrt(ms_k + EPS) * gka_ref[...], 1.0)
    kk = _rope(kk * kscale, cos, sin_signed, first_half)
    swapped = pltpu.roll(kk, HEAD_DIM, 1)
    ka_ref[...] = jnp.where(is_ka, kk, swapped).astype(ka_ref.dtype)
    ki_ref[...] = jnp.where(is_ka, swapped, kk).astype(ki_ref.dtype)

    q_i = _dot(c_q, wqi_ref[...])
    qit_ref[...] = _rope(q_i, cos_w, sin_w, first_w).T.astype(qit_ref.dtype)

    q_a = _dot(c_q, wuq_ref[...])
    sq = q_a * q_a
    sq_hi = sq.astype(hmean_ref.dtype)
    sq_lo = (sq - sq_hi.astype(jnp.float32)).astype(hmean_ref.dtype)
    ms_q = _dot(sq_hi, hmean_ref[...]) + _dot(sq_lo, hmean_ref[...])
    q_a = q_a * lax.rsqrt(ms_q + EPS) * gqa_ref[...]
    qat_ref[...] = (_rope(q_a, cos_w, sin_w, first_w) * (HEAD_DIM ** -0.5)).T.astype(qat_ref.dtype)


def _proj(x1, pos, g_mix, w_a, w_w, w_b, g_cq, w_uq, w_qi, g_qa, g_ka, tm):
    n, d = x1.shape
    dt = _MXU_DT
    inv_freq = ROPE_THETA ** (-np.arange(0, HEAD_DIM, 2, dtype=np.float64) / HEAD_DIM)
    invf = jnp.asarray(np.tile(inv_freq, LANES // (HEAD_DIM // 2)).astype(np.float32)).reshape(1, LANES)
    head_of = np.arange(A_WIDTH) // HEAD_DIM
    hmean = jnp.asarray((head_of[:, None] == head_of[None, :]).astype(np.float32) / HEAD_DIM, dt)
    gqa = jnp.tile(g_qa, A_HEADS).reshape(1, A_WIDTH)
    gka = jnp.tile(g_ka, LANES // HEAD_DIM).reshape(1, LANES)

    def full(a):
        return pl.BlockSpec(a.shape, lambda i: (0,) * a.ndim)

    def rows(w):
        return pl.BlockSpec((tm, w), lambda i: (i, 0))

    args = (x1, pos, g_mix.reshape(1, d), w_a, w_w, w_b, g_cq.reshape(1, A_Q_RANK), w_uq, w_qi,
            gqa, gka, invf, hmean)
    in_specs = [rows(d), rows(1)] + [full(a) for a in args[2:]]

    def cols(w):
        return pl.BlockSpec((w, tm), lambda i: (0, i))

    def tok(w, t):
        return jax.ShapeDtypeStruct((n, w), t)

    def keymajor(w, t=dt):
        return jax.ShapeDtypeStruct((w, n), t)

    return pl.pallas_call(
        _proj_kernel,
        grid=(n // tm,),
        in_specs=in_specs,
        out_specs=[cols(A_WIDTH), cols(A_WIDTH), rows(LANES), rows(LANES), cols(A_V_LATENT), cols(LANES),
                   rows(2 * B_WIDTH), cols(B_WIDTH)],
        out_shape=[keymajor(A_WIDTH), keymajor(A_WIDTH), tok(LANES, dt), tok(LANES, dt), keymajor(A_V_LATENT),
                   keymajor(LANES, jnp.float32), tok(2 * B_WIDTH, dt), keymajor(B_WIDTH)],
        compiler_params=_cparams(("parallel",)),
        name="proj",
    )(*args)


def _stack_heads_t(qt_ref, out_ref, tq):
    row = lax.broadcasted_iota(jnp.int32, (LANES, tq), 0)
    halves = (row < HEAD_DIM, row >= HEAD_DIM)
    for h in range(qt_ref.shape[0] // HEAD_DIM):
        pair = qt_ref[(h // 2) * LANES:(h // 2 + 1) * LANES, :]
        out_ref[:, h * tq:(h + 1) * tq] = jnp.where(halves[h % 2], pair, jnp.zeros_like(pair))


def _dsa_kernel(qit_ref, wt_ref, qat_ref, ki_ref, ka_ref, vt_ref, wuvt_ref, o_ref,
                sk_ref, qis_ref, qas_ref, acc_ref, st_ref, *, tq, top_k):
    i = pl.program_id(1)
    n_chunks = i + 1
    n_search = (n_chunks * tq + SEARCH_CHUNK - 1) // SEARCH_CHUNK
    t_col = i * tq + lax.broadcasted_iota(jnp.int32, (1, tq), 1)
    nh = A_HEADS

    def keys(c, width):
        return pl.ds(pl.multiple_of(c * width, width), width)

    _stack_heads_t(qit_ref, qis_ref, tq)
    _stack_heads_t(qat_ref, qas_ref, tq)

    def score_body(c, carry):
        d = _dot(ki_ref[keys(c, tq), :], qis_ref[...])
        s = jnp.zeros((tq, tq), jnp.float32)
        for h in range(IDX_HEADS):
            s = s + wt_ref[h:h + 1, :] * jnp.maximum(d[:, h * tq:(h + 1) * tq], 0.0)
        s = jnp.where(jnp.abs(s) < MIN_NORMAL, 0.0, s)
        bits = lax.bitcast_convert_type(s, jnp.int32)
        key = bits ^ ((bits >> 31) & 0x7FFFFFFF)
        s_pos = c * tq + lax.broadcasted_iota(jnp.int32, (tq, 1), 0)
        causal = s_pos <= t_col
        sk_ref[keys(c, tq), :] = jnp.where(causal, key, INT_MIN)
        top = lax.bitcast_convert_type(bits & TOP16, jnp.float32)
        st_ref[keys(c, tq), :] = jnp.where(causal, top, -jnp.inf).astype(st_ref.dtype)
        return carry

    lax.fori_loop(0, n_chunks, score_body, 0)

    def fill_body(c, carry):
        sk_ref[keys(c, tq), :] = jnp.full((tq, tq), INT_MIN, jnp.int32)
        st_ref[keys(c, tq), :] = jnp.full((tq, tq), -jnp.inf, st_ref.dtype)
        return carry

    lax.fori_loop(n_chunks, n_search * (SEARCH_CHUNK // tq), fill_body, 0)

    def count_ge(thr):
        def body(c, part):
            ge = jnp.where(sk_ref[keys(c, SEARCH_CHUNK), :] >= thr, 1.0, 0.0)
            ge = ge.reshape(SUBLANES, SEARCH_CHUNK // (SUBLANES * SUBLANES), SUBLANES, tq)
            return part + jnp.sum(jnp.sum(ge, axis=1), axis=0)
        part = lax.fori_loop(0, n_search, body, jnp.zeros((SUBLANES, tq), jnp.float32))
        return jnp.sum(part, axis=0, keepdims=True)

    def count_top_ge(thr_top):
        one, zero = jnp.ones((), st_ref.dtype), jnp.zeros((), st_ref.dtype)
        rows = 2 * SUBLANES

        def body(c, part):
            ge = jnp.where(st_ref[keys(c, SEARCH_CHUNK), :] >= thr_top, one, zero)
            ge = ge.reshape(SEARCH_CHUNK // rows, rows, tq)
            terms = [ge[r] for r in range(SEARCH_CHUNK // rows)]
            while len(terms) > 1:
                terms = [a + b for a, b in zip(terms[0::2], terms[1::2])]
            return part + terms[0].astype(jnp.float32)
        part = lax.fori_loop(0, n_search, body, jnp.zeros((rows, tq), jnp.float32))
        return jnp.sum(part, axis=0, keepdims=True)

    def key_to_top(cand):
        k = cand ^ INT_MIN
        fbits = k ^ ((k >> 31) & 0x7FFF0000)
        mag = fbits & 0x7FFFFFFF
        above = jnp.where(fbits < 0, 0, MIN_NORMAL_BITS)
        fbits = jnp.where(mag == 0, fbits, jnp.where(mag < MIN_NORMAL_BITS, above, fbits))
        return lax.bitcast_convert_type(fbits, jnp.float32).astype(st_ref.dtype)

    def top_bit_body(b, t_u):
        cand = t_u | lax.shift_left(jnp.int32(1), 31 - b)
        return jnp.where(count_top_ge(key_to_top(cand)) >= top_k, cand, t_u)

    def bit_body(b, t_u):
        cand = t_u | lax.shift_left(jnp.int32(1), 31 - b)
        return jnp.where(count_ge(cand ^ INT_MIN) >= top_k, cand, t_u)

    t_u = lax.fori_loop(0, 16, top_bit_body, jnp.zeros((1, tq), jnp.int32))
    t_u = lax.fori_loop(16, 32, bit_body, t_u)
    thr = jnp.maximum(t_u ^ INT_MIN, INT_MIN + 1)

    excess = count_ge(thr) - top_k

    @pl.when(jnp.max(excess) > 0.0)
    def _():
        def ties_before(limit):
            def body(c, part):
                s_pos = c * SEARCH_CHUNK + lax.broadcasted_iota(jnp.int32, (SEARCH_CHUNK, 1), 0)
                hit = jnp.where(sk_ref[keys(c, SEARCH_CHUNK), :] == thr, 1.0, 0.0)
                hit = jnp.where(s_pos < limit, hit, 0.0)
                hit = hit.reshape(SUBLANES, SEARCH_CHUNK // (SUBLANES * SUBLANES), SUBLANES, tq)
                return part + jnp.sum(jnp.sum(hit, axis=1), axis=0)
            part = lax.fori_loop(0, n_search, body, jnp.zeros((SUBLANES, tq), jnp.float32))
            return jnp.sum(part, axis=0, keepdims=True)

        n_ties = ties_before(jnp.full((1, tq), n_search * SEARCH_CHUNK, jnp.int32))
        keep = n_ties - excess
        pos_bits = sk_ref.shape[0].bit_length()

        def pos_body(b, limit):
            cand = limit | lax.shift_left(jnp.int32(1), pos_bits - 1 - b)
            return jnp.where(ties_before(cand) <= keep, cand, limit)

        limit = lax.fori_loop(0, pos_bits, pos_body, jnp.zeros((1, tq), jnp.int32))

        def demote_body(c, carry):
            s_pos = c * SEARCH_CHUNK + lax.broadcasted_iota(jnp.int32, (SEARCH_CHUNK, 1), 0)
            key = sk_ref[keys(c, SEARCH_CHUNK), :]
            sk_ref[keys(c, SEARCH_CHUNK), :] = jnp.where(key == thr, jnp.where(s_pos >= limit, INT_MIN, key), key)
            return carry

        lax.fori_loop(0, n_search, demote_body, 0)

    acc_ref[...] = jnp.zeros_like(acc_ref)

    def attn_body(c, carry):
        m_run, l_run = carry
        lg = _dot(ka_ref[keys(c, tq), :], qas_ref[...])
        bias = jnp.where(sk_ref[keys(c, tq), :] >= thr, 0.0, NEG_BIG)
        x = lg + jnp.concatenate([bias] * nh, axis=1)
        m_new = jnp.maximum(m_run, jnp.max(x, axis=0, keepdims=True))
        alpha = jnp.exp(m_run - m_new)
        p = jnp.exp(x - m_new)
        l_new = alpha * l_run + jnp.sum(p, axis=0, keepdims=True)
        acc_ref[...] = alpha * acc_ref[...] + _dot(vt_ref[:, keys(c, tq)], p.astype(vt_ref.dtype))
        return m_new, l_new

    init = (jnp.full((1, nh * tq), NEG_BIG, jnp.float32), jnp.zeros((1, nh * tq), jnp.float32))
    _, l_fin = lax.fori_loop(0, n_chunks, attn_body, init)

    o_lat = (acc_ref[...] / l_fin).astype(wuvt_ref.dtype)
    outs = [_dot(wuvt_ref[h], o_lat[:, h * tq:(h + 1) * tq]) for h in range(nh)]
    o_ref[0] = jnp.concatenate(outs, axis=0).T.astype(o_ref.dtype)


def _dsa(qit, wit, qat, ki, ka, vat, w_uv_t, bsz, top_k, tq):
    n = qit.shape[1]
    s = n // bsz
    nq = s // tq
    assert s % SEARCH_CHUNK == 0 and SEARCH_CHUNK % tq == 0
    dt = qit.dtype

    def qcols(w):
        return pl.BlockSpec((w, tq), lambda bi, i: (0, bi * nq + i))

    return pl.pallas_call(
        functools.partial(_dsa_kernel, tq=tq, top_k=top_k),
        grid=(bsz, nq),
        in_specs=[
            qcols(A_WIDTH), qcols(LANES), qcols(A_WIDTH),
            pl.BlockSpec((s, LANES), lambda bi, i: (bi, 0)),
            pl.BlockSpec((s, LANES), lambda bi, i: (bi, 0)),
            pl.BlockSpec((A_V_LATENT, s), lambda bi, i: (0, bi)),
            pl.BlockSpec((A_HEADS, HEAD_DIM, A_V_LATENT), lambda bi, i: (0, 0, 0)),
        ],
        out_specs=pl.BlockSpec((1, tq, A_WIDTH), lambda bi, i: (bi, i, 0)),
        out_shape=jax.ShapeDtypeStruct((bsz, s, A_WIDTH), dt),
        scratch_shapes=[pltpu.VMEM((s, tq), jnp.int32),
                        pltpu.VMEM((LANES, IDX_HEADS * tq), dt),
                        pltpu.VMEM((LANES, A_HEADS * tq), dt),
                        pltpu.VMEM((A_V_LATENT, A_HEADS * tq), jnp.float32),
                        pltpu.VMEM((s, tq), jnp.bfloat16)],
        compiler_params=_cparams(("parallel", "arbitrary")),
        name="dsa",
    )(qit, wit, qat, ki, ka, vat, w_uv_t)


def _stack_heads(q_ref, out_ref, tq):
    lane = lax.broadcasted_iota(jnp.int32, (tq, LANES), 1)
    halves = (lane < HEAD_DIM, lane >= HEAD_DIM)
    for h in range(q_ref.shape[-1] // HEAD_DIM):
        pair = q_ref[0, :, (h // 2) * LANES:(h // 2 + 1) * LANES]
        out_ref[h * tq:(h + 1) * tq, :] = jnp.where(halves[h % 2], pair, jnp.zeros_like(pair))


def _sb_kernel(q_ref, kt_ref, v_ref, o_ref, qs_ref, later_ref, acc_ref, *, tb):
    i = pl.program_id(1)
    row = lax.broadcasted_iota(jnp.int32, (tb, tb), 0)
    col = lax.broadcasted_iota(jnp.int32, (tb, tb), 1)
    later_ref[...] = jnp.where(row >= col, -1.0, 0.0).astype(later_ref.dtype)
    strict = col < row
    _stack_heads(q_ref, qs_ref, tb)
    acc_ref[...] = jnp.zeros_like(acc_ref)

    def block(j, carries, diag):
        ks = pl.ds(pl.multiple_of(j * tb, tb), tb)
        later = later_ref[...]
        heads = range(B_HEADS)
        pairs = [slice((h // 2) * LANES, (h // 2 + 1) * LANES) for h in heads]
        zs = [_dot(qs_ref[h * tb:(h + 1) * tb, :], kt_ref[pairs[h], ks]) for h in heads]
        splits = []
        for h in heads:
            sp = jnp.maximum(zs[h], 0.0) + jnp.log(1.0 + jnp.exp2(jnp.abs(zs[h]) * (-LOG2E)))
            if diag:
                sp = jnp.where(strict, sp, 0.0)
            splits.append(sp.astype(later.dtype))
        incls = [_dot(splits[h], later) for h in heads]
        probs, out = [], []
        for h in heads:
            a = jnp.exp2((zs[h] + incls[h] + carries[h]) * LOG2E)
            if diag:
                a = jnp.where(strict, a, 0.0)
            probs.append(a.astype(v_ref.dtype))
            out.append(carries[h] + incls[h][:, 0:1])
        for h in heads:
            acc_ref[h] += _dot(probs[h], v_ref[0, ks, pairs[h]])
        return tuple(out)

    carries = block(i, tuple(jnp.zeros((tb, 1), jnp.float32) for _ in range(B_HEADS)), True)
    lax.fori_loop(0, i, lambda k, c: block(i - 1 - k, c, False), carries)

    lane = lax.broadcasted_iota(jnp.int32, (tb, LANES), 1)
    for p in range(B_HEADS // 2):
        o_ref[0, :, p * LANES:(p + 1) * LANES] = jnp.where(
            lane < HEAD_DIM, acc_ref[2 * p], acc_ref[2 * p + 1]).astype(o_ref.dtype)


def _stickbreak(qv, kbt, tb):
    b, s, _ = qv.shape
    dt = qv.dtype
    return pl.pallas_call(
        functools.partial(_sb_kernel, tb=tb),
        grid=(b, s // tb),
        in_specs=[
            pl.BlockSpec((1, tb, B_WIDTH), lambda bi, i: (bi, i, 0)),
            pl.BlockSpec((B_WIDTH, s), lambda bi, i: (0, bi)),
            pl.BlockSpec((1, s, B_WIDTH), lambda bi, i: (bi, 0, 1)),
        ],
        out_specs=pl.BlockSpec((1, tb, B_WIDTH), lambda bi, i: (bi, i, 0)),
        out_shape=jax.ShapeDtypeStruct((b, s, B_WIDTH), dt),
        scratch_shapes=[pltpu.VMEM((B_HEADS * tb, LANES), dt),
                        pltpu.VMEM((tb, tb), dt),
                        pltpu.VMEM((B_HEADS, tb, LANES), jnp.float32)],
        compiler_params=_cparams(("parallel", "arbitrary")),
        name="stickbreak",
    )(qv, kbt, qv)


def _merge_kernel(x_ref, ya_ref, yb_ref, gmix_ref, wg_ref, woa_ref, wob_ref, wout_ref, o_ref):
    x = x_ref[...]
    d = x.shape[-1]
    h = _rms(x, gmix_ref[...]).astype(wg_ref.dtype)
    gates = _dot(h, wg_ref[...])
    merged = (_sigmoid(gates[:, :d]) * _dot(ya_ref[...], woa_ref[...])
              + _sigmoid(gates[:, d:]) * _dot(yb_ref[...], wob_ref[...]))
    o_ref[...] = x + _dot(merged.astype(wout_ref.dtype), wout_ref[...])


def _merge(x1, ya, yb, g_mix, w_g, w_oa, w_ob, w_out, tm):
    n, d = x1.shape

    def full(a):
        return pl.BlockSpec(a.shape, lambda i: (0,) * a.ndim)

    def rows(w):
        return pl.BlockSpec((tm, w), lambda i: (i, 0))

    args = (x1, ya, yb, g_mix.reshape(1, d), w_g, w_oa, w_ob, w_out)
    return pl.pallas_call(
        _merge_kernel,
        grid=(n // tm,),
        in_specs=[rows(d), rows(A_WIDTH), rows(B_WIDTH)] + [full(a) for a in args[3:]],
        out_specs=rows(d),
        out_shape=jax.ShapeDtypeStruct((n, d), jnp.float32),
        compiler_params=_cparams(("parallel",)),
        name="merge",
    )(*args)


def _pick(n, prefs):
    for p in prefs:
        if n % p == 0:
            return p
    return n


def _layer(x, positions, g_ffn1, w1_gate, w1_up, w1_down, g_mix, w_in, g_cq, w_uq_a, w_q_idx,
           g_q_a, g_k_a, w_uv_a, w_o_a, w_o_b, w_out, g_ffn2, w2_gate, w2_up, w2_down):
    bsz, seq, d = x.shape
    n = bsz * seq
    dt = _MXU_DT
    top_k = min(MAX_TOPK, seq // 4)
    tm_ffn = _pick(n, (1024, 512, 256))
    tf = _pick(w1_gate.shape[1], (256, 128))
    tm = _pick(n, (512, 256))
    tq = _pick(seq, (256, 128))
    tb = _pick(seq, (256, 128))

    xf = x.reshape(n, d)
    x1 = _ffn(xf, g_ffn1, w1_gate.astype(dt), w1_up.astype(dt), (0.5 * w1_down).astype(dt), tm_ffn, tf)

    o = np.cumsum((A_Q_RANK, HEAD_DIM, A_V_LATENT, IDX_DIM, IDX_HEADS, B_WIDTH, B_WIDTH, B_WIDTH, d, d))
    w_a = jnp.concatenate([w_in[:, :o[0]], w_in[:, o[0]:o[1]], w_in[:, o[2]:o[3]], w_in[:, o[1]:o[2]]],
                          axis=1).astype(dt)
    w_w = jnp.pad(w_in[:, o[3]:o[4]], ((0, 0), (0, LANES - IDX_HEADS))).astype(dt)
    w_b = jnp.concatenate([w_in[:, o[4]:o[5]] * (HEAD_DIM ** -0.5), w_in[:, o[5]:o[7]]], axis=1).astype(dt)
    w_g = w_in[:, o[7]:].astype(dt)

    qat, qit, ka, ki, vat, wit, qv, kbt = _proj(
        x1, positions.reshape(n, 1), g_mix, w_a, w_w, w_b, g_cq, w_uq_a.astype(dt), w_q_idx.astype(dt),
        g_q_a, g_k_a, tm)

    ya = _dsa(qit, wit, qat, ki, ka, vat, w_uv_a.transpose(0, 2, 1).astype(dt), bsz, top_k, tq)
    yb = _stickbreak(qv.reshape(bsz, seq, 2 * B_WIDTH), kbt, tb)

    x2 = _merge(x1, ya.reshape(n, A_WIDTH), yb.reshape(n, B_WIDTH), g_mix, w_g, w_o_a.astype(dt),
                w_o_b.astype(dt), w_out.astype(dt), tm)
    x3 = _ffn(x2, g_ffn2, w2_gate.astype(dt), w2_up.astype(dt), (0.5 * w2_down).astype(dt), tm_ffn, tf)
    return x3.reshape(bsz, seq, d)


def kernel(x, positions, g_ffn1, w1_gate, w1_up, w1_down, g_mix, w_in, g_cq, w_uq_a, w_q_idx, g_q_a, g_k_a,
           w_uv_a, w_o_a, w_o_b, w_out, g_ffn2, w2_gate, w2_up, w2_down):
    depth = g_ffn1.shape[0]
    for l in range(depth):
        x = _layer(x, positions, g_ffn1[l], w1_gate[l], w1_up[l], w1_down[l], g_mix[l], w_in[l], g_cq[l],
                   w_uq_a[l], w_q_idx[l], g_q_a[l], g_k_a[l], w_uv_a[l], w_o_a[l], w_o_b[l], w_out[l],
                   g_ffn2[l], w2_gate[l], w2_up[l], w2_down[l])
    return x
```

```python
import functools

import numpy as np
import jax
import jax.numpy as jnp
from jax import lax
from jax.experimental import pallas as pl
from jax.experimental.pallas import tpu as pltpu

HEAD_DIM = 64
A_HEADS = 8
A_Q_RANK = 256
A_V_LATENT = 128
IDX_HEADS = 8
IDX_DIM = 64
B_HEADS = 8
MAX_TOPK = 256
ROPE_THETA = 10000.0
EPS = 1e-6
A_WIDTH = A_HEADS * HEAD_DIM
B_WIDTH = B_HEADS * HEAD_DIM
LANES = 128
SUBLANES = 8
INT_MIN = -(2 ** 31)
NEG_BIG = -1e30
LOG2E = 1.4426950408889634
SEARCH_CHUNK = 512

_MXU_DT = jnp.bfloat16
_VMEM_LIMIT = 56 * 1024 * 1024
TOP16 = -65536
MIN_NORMAL_BITS = 0x00800000
MIN_NORMAL = float(np.float32(2.0) ** -126)


def _cparams(sem):
    return pltpu.CompilerParams(dimension_semantics=sem, vmem_limit_bytes=_VMEM_LIMIT)


def _dot(a, b):
    return jnp.dot(a, b, preferred_element_type=jnp.float32)


def _sigmoid(x):
    return 1.0 / (1.0 + jnp.exp(-x))


def _rms(x, g):
    ms = jnp.mean(x * x, axis=-1, keepdims=True)
    return x * lax.rsqrt(ms + EPS) * g


def _ffn_kernel(x_ref, g_ref, wg_ref, wu_ref, wd_ref, o_ref, h_ref):
    @pl.when(pl.program_id(1) == 0)
    def _():
        x = x_ref[...]
        h_ref[...] = _rms(x, g_ref[...]).astype(h_ref.dtype)
        o_ref[...] = x

    h = h_ref[...]
    gate = _dot(h, wg_ref[0])
    up = _dot(h, wu_ref[0])
    act = (gate * _sigmoid(gate) * up).astype(wd_ref.dtype)
    o_ref[...] += _dot(act, wd_ref[...])


def _ffn(x, g, wg, wu, wd, tm, tf):
    n, d = x.shape
    dff = wg.shape[1]
    nf = dff // tf

    def tiles(w):
        return w.reshape(d, nf, tf).transpose(1, 0, 2)

    return pl.pallas_call(
        _ffn_kernel,
        grid=(n // tm, nf),
        in_specs=[
            pl.BlockSpec((tm, d), lambda i, f: (i, 0)),
            pl.BlockSpec((1, d), lambda i, f: (0, 0)),
            pl.BlockSpec((1, d, tf), lambda i, f: (f, 0, 0)),
            pl.BlockSpec((1, d, tf), lambda i, f: (f, 0, 0)),
            pl.BlockSpec((tf, d), lambda i, f: (f, 0)),
        ],
        out_specs=pl.BlockSpec((tm, d), lambda i, f: (i, 0)),
        out_shape=jax.ShapeDtypeStruct((n, d), jnp.float32),
        scratch_shapes=[pltpu.VMEM((tm, d), _MXU_DT)],
        compiler_params=_cparams(("parallel", "arbitrary")),
        name="ffn",
    )(x, g.reshape(1, d), tiles(wg), tiles(wu), wd)


def _rope(x, cos, sin_signed, first_half):
    n = x.shape[-1]
    half = HEAD_DIM // 2
    fwd = pltpu.roll(x, n - half, 1)
    bwd = pltpu.roll(x, half, 1)
    return x * cos + jnp.where(first_half, fwd, bwd) * sin_signed


def _proj_kernel(x_ref, pos_ref, gmix_ref, wa_ref, ww_ref, wb_ref, gcq_ref, wuq_ref, wqi_ref,
                 gqa_ref, gka_ref, invf_ref, hmean_ref,
                 qat_ref, qit_ref, ka_ref, ki_ref, vat_ref, wit_ref, qv_ref, kbt_ref):
    h = _rms(x_ref[...], gmix_ref[...]).astype(wa_ref.dtype)

    r = _dot(h, wb_ref[...])
    qv_ref[:, :B_WIDTH] = r[:, :B_WIDTH].astype(qv_ref.dtype)
    qv_ref[:, B_WIDTH:] = r[:, 2 * B_WIDTH:].astype(qv_ref.dtype)
    kbt_ref[...] = r[:, B_WIDTH:2 * B_WIDTH].T.astype(kbt_ref.dtype)
    wit_ref[...] = (_dot(h, ww_ref[...]) * ((IDX_HEADS ** -0.5) * (IDX_DIM ** -0.5))).T

    pa = _dot(h, wa_ref[...])
    c_q = _rms(pa[:, :A_Q_RANK], gcq_ref[...]).astype(wuq_ref.dtype)
    kk = pa[:, A_Q_RANK:A_Q_RANK + LANES]
    vat_ref[...] = pa[:, A_Q_RANK + LANES:].T.astype(vat_ref.dtype)

    ang = pos_ref[...].astype(jnp.float32) * invf_ref[...]
    cos = jnp.cos(ang)
    sin = jnp.sin(ang)
    lane = lax.broadcasted_iota(jnp.int32, cos.shape, 1)
    first_half = (lane & (HEAD_DIM // 2)) == 0
    sin_signed = jnp.where(first_half, -sin, sin)
    reps = A_WIDTH // LANES
    cos_w = jnp.concatenate([cos] * reps, axis=1)
    sin_w = jnp.concatenate([sin_signed] * reps, axis=1)
    first_w = (lax.broadcasted_iota(jnp.int32, cos_w.shape, 1) & (HEAD_DIM // 2)) == 0

    is_ka = lane < HEAD_DIM
    ms_k = jnp.sum(jnp.where(is_ka, kk * kk, 0.0), axis=-1, keepdims=True) * (1.0 / HEAD_DIM)
    kscale = jnp.where(is_ka, lax.rsqrt(ms_k + EPS) * gka_ref[...], 1.0)
    kk = _rope(kk * kscale, cos, sin_signed, first_half)
    swapped = pltpu.roll(kk, HEAD_DIM, 1)
    ka_ref[...] = jnp.where(is_ka, kk, swapped).astype(ka_ref.dtype)
    ki_ref[...] = jnp.where(is_ka, swapped, kk).astype(ki_ref.dtype)

    q_i = _dot(c_q, wqi_ref[...])
    qit_ref[...] = _rope(q_i, cos_w, sin_w, first_w).T.astype(qit_ref.dtype)

    q_a = _dot(c_q, wuq_ref[...])
    sq = q_a * q_a
    sq_hi = sq.astype(hmean_ref.dtype)
    sq_lo = (sq - sq_hi.astype(jnp.float32)).astype(hmean_ref.dtype)
    ms_q = _dot(sq_hi, hmean_ref[...]) + _dot(sq_lo, hmean_ref[...])
    q_a = q_a * lax.rsqrt(ms_q + EPS) * gqa_ref[...]
    qat_ref[...] = (_rope(q_a, cos_w, sin_w, first_w) * (HEAD_DIM ** -0.5)).T.astype(qat_ref.dtype)


def _proj(x1, pos, g_mix, w_a, w_w, w_b, g_cq, w_uq, w_qi, g_qa, g_ka, tm):
    n, d = x1.shape
    dt = _MXU_DT
    inv_freq = ROPE_THETA ** (-np.arange(0, HEAD_DIM, 2, dtype=np.float64) / HEAD_DIM)
    invf = jnp.asarray(np.tile(inv_freq, LANES // (HEAD_DIM // 2)).astype(np.float32)).reshape(1, LANES)
    head_of = np.arange(A_WIDTH) // HEAD_DIM
    hmean = jnp.asarray((head_of[:, None] == head_of[None, :]).astype(np.float32) / HEAD_DIM, dt)
    gqa = jnp.tile(g_qa, A_HEADS).reshape(1, A_WIDTH)
    gka = jnp.tile(g_ka, LANES // HEAD_DIM).reshape(1, LANES)

    def full(a):
        return pl.BlockSpec(a.shape, lambda i: (0,) * a.ndim)

    def rows(w):
        return pl.BlockSpec((tm, w), lambda i: (i, 0))

    args = (x1, pos, g_mix.reshape(1, d), w_a, w_w, w_b, g_cq.reshape(1, A_Q_RANK), w_uq, w_qi,
            gqa, gka, invf, hmean)
    in_specs = [rows(d), rows(1)] + [full(a) for a in args[2:]]

    def cols(w):
        return pl.BlockSpec((w, tm), lambda i: (0, i))

    def tok(w, t):
        return jax.ShapeDtypeStruct((n, w), t)

    def keymajor(w, t=dt):
        return jax.ShapeDtypeStruct((w, n), t)

    return pl.pallas_call(
        _proj_kernel,
        grid=(n // tm,),
        in_specs=in_specs,
        out_specs=[cols(A_WIDTH), cols(A_WIDTH), rows(LANES), rows(LANES), cols(A_V_LATENT), cols(LANES),
                   rows(2 * B_WIDTH), cols(B_WIDTH)],
        out_shape=[keymajor(A_WIDTH), keymajor(A_WIDTH), tok(LANES, dt), tok(LANES, dt), keymajor(A_V_LATENT),
                   keymajor(LANES, jnp.float32), tok(2 * B_WIDTH, dt), keymajor(B_WIDTH)],
        compiler_params=_cparams(("parallel",)),
        name="proj",
    )(*args)


def _stack_heads_t(qt_ref, out_ref, tq):
    row = lax.broadcasted_iota(jnp.int32, (LANES, tq), 0)
    halves = (row < HEAD_DIM, row >= HEAD_DIM)
    for h in range(qt_ref.shape[0] // HEAD_DIM):
        pair = qt_ref[(h // 2) * LANES:(h // 2 + 1) * LANES, :]
        out_ref[:, h * tq:(h + 1) * tq] = jnp.where(halves[h % 2], pair, jnp.zeros_like(pair))


def _dsa_kernel(qit_ref, wt_ref, qat_ref, ki_ref, ka_ref, vt_ref, wuvt_ref, o_ref,
                sk_ref, qis_ref, qas_ref, acc_ref, st_ref, *, tq, top_k):
    i = pl.program_id(1)
    n_chunks = i + 1
    n_search = (n_chunks * tq + SEARCH_CHUNK - 1) // SEARCH_CHUNK
    t_col = i * tq + lax.broadcasted_iota(jnp.int32, (1, tq), 1)
    nh = A_HEADS

    def keys(c, width):
        return pl.ds(pl.multiple_of(c * width, width), width)

    _stack_heads_t(qit_ref, qis_ref, tq)
    _stack_heads_t(qat_ref, qas_ref, tq)

    def score_chunk(c):
        d = _dot(ki_ref[keys(c, tq), :], qis_ref[...])
        s = jnp.zeros((tq, tq), jnp.float32)
        for h in range(IDX_HEADS):
            s = s + wt_ref[h:h + 1, :] * jnp.maximum(d[:, h * tq:(h + 1) * tq], 0.0)
        s = jnp.where(jnp.abs(s) < MIN_NORMAL, 0.0, s)
        bits = lax.bitcast_convert_type(s, jnp.int32)
        key = bits ^ ((bits >> 31) & 0x7FFFFFFF)
        s_pos = c * tq + lax.broadcasted_iota(jnp.int32, (tq, 1), 0)
        causal = s_pos <= t_col
        sk_ref[keys(c, tq), :] = jnp.where(causal, key, INT_MIN)
        top = lax.bitcast_convert_type(bits & TOP16, jnp.float32)
        st_ref[keys(c, tq), :] = jnp.where(causal, top, -jnp.inf).astype(st_ref.dtype)

    def score_body(c2, carry):
        score_chunk(2 * c2)
        score_chunk(2 * c2 + 1)
        return carry

    lax.fori_loop(0, n_search, score_body, 0)

    def count_ge(thr):
        def body(c, part):
            ge = jnp.where(sk_ref[keys(c, SEARCH_CHUNK), :] >= thr, 1.0, 0.0)
            ge = ge.reshape(SUBLANES, SEARCH_CHUNK // (SUBLANES * SUBLANES), SUBLANES, tq)
            return part + jnp.sum(jnp.sum(ge, axis=1), axis=0)
        part = lax.fori_loop(0, n_search, body, jnp.zeros((SUBLANES, tq), jnp.float32))
        return jnp.sum(part, axis=0, keepdims=True)

    def count_top_ge(thr_top):
        one, zero = jnp.ones((), st_ref.dtype), jnp.zeros((), st_ref.dtype)
        rows = 2 * SUBLANES

        def body(c, part):
            ge = jnp.where(st_ref[keys(c, SEARCH_CHUNK), :] >= thr_top, one, zero)
            ge = ge.reshape(SEARCH_CHUNK // rows, rows, tq)
            terms = [ge[r] for r in range(SEARCH_CHUNK // rows)]
            while len(terms) > 1:
                terms = [a + b for a, b in zip(terms[0::2], terms[1::2])]
            return part + terms[0].astype(jnp.float32)
        part = lax.fori_loop(0, n_search, body, jnp.zeros((rows, tq), jnp.float32))
        return jnp.sum(part, axis=0, keepdims=True)

    def key_to_top(cand):
        k = cand ^ INT_MIN
        fbits = k ^ ((k >> 31) & 0x7FFF0000)
        mag = fbits & 0x7FFFFFFF
        above = jnp.where(fbits < 0, 0, MIN_NORMAL_BITS)
        fbits = jnp.where(mag == 0, fbits, jnp.where(mag < MIN_NORMAL_BITS, above, fbits))
        return lax.bitcast_convert_type(fbits, jnp.float32).astype(st_ref.dtype)

    def top_bit_body(b, t_u):
        cand = t_u | lax.shift_left(jnp.int32(1), 31 - b)
        return jnp.where(count_top_ge(key_to_top(cand)) >= top_k, cand, t_u)

    def bit_body(b, t_u):
        cand = t_u | lax.shift_left(jnp.int32(1), 31 - b)
        return jnp.where(count_ge(cand ^ INT_MIN) >= top_k, cand, t_u)

    t_u = lax.fori_loop(0, 16, top_bit_body, jnp.zeros((1, tq), jnp.int32))
    t_u = lax.fori_loop(16, 32, bit_body, t_u)
    thr = jnp.maximum(t_u ^ INT_MIN, INT_MIN + 1)

    excess = count_ge(thr) - top_k

    @pl.when(jnp.max(excess) > 0.0)
    def _():
        def ties_before(limit):
            def body(c, part):
                s_pos = c * SEARCH_CHUNK + lax.broadcasted_iota(jnp.int32, (SEARCH_CHUNK, 1), 0)
                hit = jnp.where(sk_ref[keys(c, SEARCH_CHUNK), :] == thr, 1.0, 0.0)
                hit = jnp.where(s_pos < limit, hit, 0.0)
                hit = hit.reshape(SUBLANES, SEARCH_CHUNK // (SUBLANES * SUBLANES), SUBLANES, tq)
                return part + jnp.sum(jnp.sum(hit, axis=1), axis=0)
            part = lax.fori_loop(0, n_search, body, jnp.zeros((SUBLANES, tq), jnp.float32))
            return jnp.sum(part, axis=0, keepdims=True)

        n_ties = ties_before(jnp.full((1, tq), n_search * SEARCH_CHUNK, jnp.int32))
        keep = n_ties - excess
        pos_bits = sk_ref.shape[0].bit_length()

        def pos_body(b, limit):
            cand = limit | lax.shift_left(jnp.int32(1), pos_bits - 1 - b)
            return jnp.where(ties_before(cand) <= keep, cand, limit)

        limit = lax.fori_loop(0, pos_bits, pos_body, jnp.zeros((1, tq), jnp.int32))

        def demote_body(c, carry):
            s_pos = c * SEARCH_CHUNK + lax.broadcasted_iota(jnp.int32, (SEARCH_CHUNK, 1), 0)
            key = sk_ref[keys(c, SEARCH_CHUNK), :]
            sk_ref[keys(c, SEARCH_CHUNK), :] = jnp.where(key == thr, jnp.where(s_pos >= limit, INT_MIN, key), key)
            return carry

        lax.fori_loop(0, n_search, demote_body, 0)

    acc_ref[...] = jnp.zeros_like(acc_ref)

    def attn_body(c, carry):
        m_run, l_run = carry
        lg = _dot(ka_ref[keys(c, tq), :], qas_ref[...])
        bias = jnp.where(sk_ref[keys(c, tq), :] >= thr, 0.0, NEG_BIG)
        x = lg + jnp.concatenate([bias] * nh, axis=1)
        m_new = jnp.maximum(m_run, jnp.max(x, axis=0, keepdims=True))
        alpha = jnp.exp(m_run - m_new)
        p = jnp.exp(x - m_new)
        l_new = alpha * l_run + jnp.sum(p, axis=0, keepdims=True)
        acc_ref[...] = alpha * acc_ref[...] + _dot(vt_ref[:, keys(c, tq)], p.astype(vt_ref.dtype))
        return m_new, l_new

    init = (jnp.full((1, nh * tq), NEG_BIG, jnp.float32), jnp.zeros((1, nh * tq), jnp.float32))
    _, l_fin = lax.fori_loop(0, n_chunks, attn_body, init)

    o_lat = (acc_ref[...] / l_fin).astype(wuvt_ref.dtype)
    outs = [_dot(wuvt_ref[h], o_lat[:, h * tq:(h + 1) * tq]) for h in range(nh)]
    o_ref[0] = jnp.concatenate(outs, axis=0).T.astype(o_ref.dtype)


def _dsa(qit, wit, qat, ki, ka, vat, w_uv_t, bsz, top_k, tq):
    n = qit.shape[1]
    s = n // bsz
    nq = s // tq
    assert s % SEARCH_CHUNK == 0 and SEARCH_CHUNK == 2 * tq
    dt = qit.dtype

    def qcols(w):
        return pl.BlockSpec((w, tq), lambda bi, i: (0, bi * nq + i))

    return pl.pallas_call(
        functools.partial(_dsa_kernel, tq=tq, top_k=top_k),
        grid=(bsz, nq),
        in_specs=[
            qcols(A_WIDTH), qcols(LANES), qcols(A_WIDTH),
            pl.BlockSpec((s, LANES), lambda bi, i: (bi, 0)),
            pl.BlockSpec((s, LANES), lambda bi, i: (bi, 0)),
            pl.BlockSpec((A_V_LATENT, s), lambda bi, i: (0, bi)),
            pl.BlockSpec((A_HEADS, HEAD_DIM, A_V_LATENT), lambda bi, i: (0, 0, 0)),
        ],
        out_specs=pl.BlockSpec((1, tq, A_WIDTH), lambda bi, i: (bi, i, 0)),
        out_shape=jax.ShapeDtypeStruct((bsz, s, A_WIDTH), dt),
        scratch_shapes=[pltpu.VMEM((s, tq), jnp.int32),
                        pltpu.VMEM((LANES, IDX_HEADS * tq), dt),
                        pltpu.VMEM((LANES, A_HEADS * tq), dt),
                        pltpu.VMEM((A_V_LATENT, A_HEADS * tq), jnp.float32),
                        pltpu.VMEM((s, tq), jnp.bfloat16)],
        compiler_params=_cparams(("parallel", "arbitrary")),
        name="dsa",
    )(qit, wit, qat, ki, ka, vat, w_uv_t)


def _stack_heads(q_ref, out_ref, tq):
    lane = lax.broadcasted_iota(jnp.int32, (tq, LANES), 1)
    halves = (lane < HEAD_DIM, lane >= HEAD_DIM)
    for h in range(q_ref.shape[-1] // HEAD_DIM):
        pair = q_ref[0, :, (h // 2) * LANES:(h // 2 + 1) * LANES]
        out_ref[h * tq:(h + 1) * tq, :] = jnp.where(halves[h % 2], pair, jnp.zeros_like(pair))


def _sb_kernel(q_ref, kt_ref, v_ref, o_ref, qs_ref, later_ref, acc_ref, *, tb):
    i = pl.program_id(1)
    row = lax.broadcasted_iota(jnp.int32, (tb, tb), 0)
    col = lax.broadcasted_iota(jnp.int32, (tb, tb), 1)
    later_ref[...] = jnp.where(row >= col, -1.0, 0.0).astype(later_ref.dtype)
    strict = col < row
    _stack_heads(q_ref, qs_ref, tb)
    acc_ref[...] = jnp.zeros_like(acc_ref)

    def block(j, carries, diag):
        ks = pl.ds(pl.multiple_of(j * tb, tb), tb)
        later = later_ref[...]
        zs, incls, out = {}, {}, [None] * B_HEADS

        def pair(h):
            return slice((h // 2) * LANES, (h // 2 + 1) * LANES)

        def logits(h):
            zs[h] = _dot(qs_ref[h * tb:(h + 1) * tb, :], kt_ref[pair(h), ks])

        def cumsum(h):
            sp = jnp.maximum(zs[h], 0.0) + jnp.log(1.0 + jnp.exp2(jnp.abs(zs[h]) * (-LOG2E)))
            if diag:
                sp = jnp.where(strict, sp, 0.0)
            incls[h] = _dot(sp.astype(later.dtype), later)

        def weights(h):
            a = jnp.exp2((zs[h] + incls[h] + carries[h]) * LOG2E)
            if diag:
                a = jnp.where(strict, a, 0.0)
            acc_ref[h] += _dot(a.astype(v_ref.dtype), v_ref[0, ks, pair(h)])
            out[h] = carries[h] + incls[h][:, 0:1]

        for step in range(B_HEADS + 2):
            if step < B_HEADS:
                logits(step)
            if 0 <= step - 1 < B_HEADS:
                cumsum(step - 1)
            if 0 <= step - 2 < B_HEADS:
                weights(step - 2)
        return tuple(out)

    carries = block(i, tuple(jnp.zeros((tb, 1), jnp.float32) for _ in range(B_HEADS)), True)
    lax.fori_loop(0, i, lambda k, c: block(i - 1 - k, c, False), carries)

    lane = lax.broadcasted_iota(jnp.int32, (tb, LANES), 1)
    for p in range(B_HEADS // 2):
        o_ref[0, :, p * LANES:(p + 1) * LANES] = jnp.where(
            lane < HEAD_DIM, acc_ref[2 * p], acc_ref[2 * p + 1]).astype(o_ref.dtype)


def _stickbreak(qv, kbt, tb):
    b, s, _ = qv.shape
    dt = qv.dtype
    return pl.pallas_call(
        functools.partial(_sb_kernel, tb=tb),
        grid=(b, s // tb),
        in_specs=[
            pl.BlockSpec((1, tb, B_WIDTH), lambda bi, i: (bi, i, 0)),
            pl.BlockSpec((B_WIDTH, s), lambda bi, i: (0, bi)),
            pl.BlockSpec((1, s, B_WIDTH), lambda bi, i: (bi, 0, 1)),
        ],
        out_specs=pl.BlockSpec((1, tb, B_WIDTH), lambda bi, i: (bi, i, 0)),
        out_shape=jax.ShapeDtypeStruct((b, s, B_WIDTH), dt),
        scratch_shapes=[pltpu.VMEM((B_HEADS * tb, LANES), dt),
                        pltpu.VMEM((tb, tb), dt),
                        pltpu.VMEM((B_HEADS, tb, LANES), jnp.float32)],
        compiler_params=_cparams(("parallel", "arbitrary")),
        name="stickbreak",
    )(qv, kbt, qv)


def _merge_kernel(x_ref, ya_ref, yb_ref, gmix_ref, wg_ref, woa_ref, wob_ref, wout_ref, o_ref):
    x = x_ref[...]
    d = x.shape[-1]
    h = _rms(x, gmix_ref[...]).astype(wg_ref.dtype)
    gates = _dot(h, wg_ref[...])
    merged = (_sigmoid(gates[:, :d]) * _dot(ya_ref[...], woa_ref[...])
              + _sigmoid(gates[:, d:]) * _dot(yb_ref[...], wob_ref[...]))
    o_ref[...] = x + _dot(merged.astype(wout_ref.dtype), wout_ref[...])


def _merge(x1, ya, yb, g_mix, w_g, w_oa, w_ob, w_out, tm):
    n, d = x1.shape

    def full(a):
        return pl.BlockSpec(a.shape, lambda i: (0,) * a.ndim)

    def rows(w):
        return pl.BlockSpec((tm, w), lambda i: (i, 0))

    args = (x1, ya, yb, g_mix.reshape(1, d), w_g, w_oa, w_ob, w_out)
    return pl.pallas_call(
        _merge_kernel,
        grid=(n // tm,),
        in_specs=[rows(d), rows(A_WIDTH), rows(B_WIDTH)] + [full(a) for a in args[3:]],
        out_specs=rows(d),
        out_shape=jax.ShapeDtypeStruct((n, d), jnp.float32),
        compiler_params=_cparams(("parallel",)),
        name="merge",
    )(*args)


def _pick(n, prefs):
    for p in prefs:
        if n % p == 0:
            return p
    return n


def _layer(x, positions, g_ffn1, w1_gate, w1_up, w1_down, g_mix, w_in, g_cq, w_uq_a, w_q_idx,
           g_q_a, g_k_a, w_uv_a, w_o_a, w_o_b, w_out, g_ffn2, w2_gate, w2_up, w2_down):
    bsz, seq, d = x.shape
    n = bsz * seq
    dt = _MXU_DT
    top_k = min(MAX_TOPK, seq // 4)
    tm_ffn = _pick(n, (1024, 512, 256))
    tf = _pick(w1_gate.shape[1], (256, 128))
    tm = _pick(n, (512, 256))
    tq = _pick(seq, (256, 128))
    tb = _pick(seq, (256, 128))

    xf = x.reshape(n, d)
    x1 = _ffn(xf, g_ffn1, w1_gate.astype(dt), w1_up.astype(dt), (0.5 * w1_down).astype(dt), tm_ffn, tf)

    o = np.cumsum((A_Q_RANK, HEAD_DIM, A_V_LATENT, IDX_DIM, IDX_HEADS, B_WIDTH, B_WIDTH, B_WIDTH, d, d))
    w_a = jnp.concatenate([w_in[:, :o[0]], w_in[:, o[0]:o[1]], w_in[:, o[2]:o[3]], w_in[:, o[1]:o[2]]],
                          axis=1).astype(dt)
    w_w = jnp.pad(w_in[:, o[3]:o[4]], ((0, 0), (0, LANES - IDX_HEADS))).astype(dt)
    w_b = jnp.concatenate([w_in[:, o[4]:o[5]] * (HEAD_DIM ** -0.5), w_in[:, o[5]:o[7]]], axis=1).astype(dt)
    w_g = w_in[:, o[7]:].astype(dt)

    qat, qit, ka, ki, vat, wit, qv, kbt = _proj(
        x1, positions.reshape(n, 1), g_mix, w_a, w_w, w_b, g_cq, w_uq_a.astype(dt), w_q_idx.astype(dt),
        g_q_a, g_k_a, tm)

    ya = _dsa(qit, wit, qat, ki, ka, vat, w_uv_a.transpose(0, 2, 1).astype(dt), bsz, top_k, tq)
    yb = _stickbreak(qv.reshape(bsz, seq, 2 * B_WIDTH), kbt, tb)

    x2 = _merge(x1, ya.reshape(n, A_WIDTH), yb.reshape(n, B_WIDTH), g_mix, w_g, w_o_a.astype(dt),
                w_o_b.astype(dt), w_out.astype(dt), tm)
    x3 = _ffn(x2, g_ffn2, w2_gate.astype(dt), w2_up.astype(dt), (0.5 * w2_down).astype(dt), tm_ffn, tf)
    return x3.reshape(bsz, seq, d)


def kernel(x, positions, g_ffn1, w1_gate, w1_up, w1_down, g_mix, w_in, g_cq, w_uq_a, w_q_idx, g_q_a, g_k_a,
           w_uv_a, w_o_a, w_o_b, w_out, g_ffn2, w2_gate, w2_up, w2_down):
    depth = g_ffn1.shape[0]
    for l in range(depth):
        x = _layer(x, positions, g_ffn1[l], w1_gate[l], w1_up[l], w1_down[l], g_mix[l], w_in[l], g_cq[l],
                   w_uq_a[l], w_q_idx[l], g_q_a[l], g_k_a[l], w_uv_a[l], w_o_a[l], w_o_b[l], w_out[l],
                   g_ffn2[l], w2_gate[l], w2_up[l], w2_down[l])
    return x
```

```python
import functools

import numpy as np
import jax
import jax.numpy as jnp
from jax import lax
from jax.experimental import pallas as pl
from jax.experimental.pallas import tpu as pltpu

HEAD_DIM = 64
A_HEADS = 8
A_Q_RANK = 256
A_V_LATENT = 128
IDX_HEADS = 8
IDX_DIM = 64
B_HEADS = 8
MAX_TOPK = 256
ROPE_THETA = 10000.0
EPS = 1e-6
A_WIDTH = A_HEADS * HEAD_DIM
B_WIDTH = B_HEADS * HEAD_DIM
LANES = 128
SUBLANES = 8
INT_MIN = -(2 ** 31)
NEG_BIG = -1e30
LOG2E = 1.4426950408889634
SEARCH_CHUNK = 512

_MXU_DT = jnp.bfloat16
_VMEM_LIMIT = 56 * 1024 * 1024
TOP16 = -65536
MIN_NORMAL_BITS = 0x00800000
MIN_NORMAL = float(np.float32(2.0) ** -126)


def _cparams(sem):
    return pltpu.CompilerParams(dimension_semantics=sem, vmem_limit_bytes=_VMEM_LIMIT)


def _dot(a, b):
    return jnp.dot(a, b, preferred_element_type=jnp.float32)


def _sigmoid(x):
    return 1.0 / (1.0 + jnp.exp(-x))


def _rms(x, g):
    ms = jnp.mean(x * x, axis=-1, keepdims=True)
    return x * lax.rsqrt(ms + EPS) * g


def _ffn_kernel(x_ref, g_ref, wg_ref, wu_ref, wd_ref, o_ref, h_ref):
    @pl.when(pl.program_id(1) == 0)
    def _():
        x = x_ref[...]
        h_ref[...] = _rms(x, g_ref[...]).astype(h_ref.dtype)
        o_ref[...] = x

    h = h_ref[...]
    gate = _dot(h, wg_ref[...])
    up = _dot(h, wu_ref[...])
    act = (gate * _sigmoid(gate) * up).astype(wd_ref.dtype)
    o_ref[...] += _dot(act, wd_ref[...])


def _ffn(x, g, wg, wu, wd, tm, tf):
    n, d = x.shape
    dff = wg.shape[1]
    return pl.pallas_call(
        _ffn_kernel,
        grid=(n // tm, dff // tf),
        in_specs=[
            pl.BlockSpec((tm, d), lambda i, f: (i, 0)),
            pl.BlockSpec((1, d), lambda i, f: (0, 0)),
            pl.BlockSpec((d, tf), lambda i, f: (0, f)),
            pl.BlockSpec((d, tf), lambda i, f: (0, f)),
            pl.BlockSpec((tf, d), lambda i, f: (f, 0)),
        ],
        out_specs=pl.BlockSpec((tm, d), lambda i, f: (i, 0)),
        out_shape=jax.ShapeDtypeStruct((n, d), jnp.float32),
        scratch_shapes=[pltpu.VMEM((tm, d), _MXU_DT)],
        compiler_params=_cparams(("parallel", "arbitrary")),
        name="ffn",
    )(x, g.reshape(1, d), wg, wu, wd)


def _rope(x, cos, sin_signed, first_half):
    n = x.shape[-1]
    half = HEAD_DIM // 2
    fwd = pltpu.roll(x, n - half, 1)
    bwd = pltpu.roll(x, half, 1)
    return x * cos + jnp.where(first_half, fwd, bwd) * sin_signed


def _proj_kernel(x_ref, pos_ref, gmix_ref, wa_ref, ww_ref, wb_ref, gcq_ref, wuq_ref, wqi_ref,
                 gqa_ref, gka_ref, invf_ref, hmean_ref,
                 qat_ref, qit_ref, ka_ref, ki_ref, vat_ref, wit_ref, qv_ref, kbt_ref):
    h = _rms(x_ref[...], gmix_ref[...]).astype(wa_ref.dtype)

    r = _dot(h, wb_ref[...])
    qv_ref[:, :B_WIDTH] = r[:, :B_WIDTH].astype(qv_ref.dtype)
    qv_ref[:, B_WIDTH:] = r[:, 2 * B_WIDTH:].astype(qv_ref.dtype)
    kbt_ref[...] = r[:, B_WIDTH:2 * B_WIDTH].T.astype(kbt_ref.dtype)
    wit_ref[...] = (_dot(h, ww_ref[...]) * ((IDX_HEADS ** -0.5) * (IDX_DIM ** -0.5))).T

    pa = _dot(h, wa_ref[...])
    c_q = _rms(pa[:, :A_Q_RANK], gcq_ref[...]).astype(wuq_ref.dtype)
    kk = pa[:, A_Q_RANK:A_Q_RANK + LANES]
    vat_ref[...] = pa[:, A_Q_RANK + LANES:].T.astype(vat_ref.dtype)

    ang = pos_ref[...].astype(jnp.float32) * invf_ref[...]
    cos = jnp.cos(ang)
    sin = jnp.sin(ang)
    lane = lax.broadcasted_iota(jnp.int32, cos.shape, 1)
    first_half = (lane & (HEAD_DIM // 2)) == 0
    sin_signed = jnp.where(first_half, -sin, sin)
    reps = A_WIDTH // LANES
    cos_w = jnp.concatenate([cos] * reps, axis=1)
    sin_w = jnp.concatenate([sin_signed] * reps, axis=1)
    first_w = (lax.broadcasted_iota(jnp.int32, cos_w.shape, 1) & (HEAD_DIM // 2)) == 0

    is_ka = lane < HEAD_DIM
    ms_k = jnp.sum(jnp.where(is_ka, kk * kk, 0.0), axis=-1, keepdims=True) * (1.0 / HEAD_DIM)
    kscale = jnp.where(is_ka, lax.rsqrt(ms_k + EPS) * gka_ref[...], 1.0)
    kk = _rope(kk * kscale, cos, sin_signed, first_half)
    swapped = pltpu.roll(kk, HEAD_DIM, 1)
    ka_ref[...] = jnp.where(is_ka, kk, swapped).astype(ka_ref.dtype)
    ki_ref[...] = jnp.where(is_ka, swapped, kk).astype(ki_ref.dtype)

    q_i = _dot(c_q, wqi_ref[...])
    qit_ref[...] = _rope(q_i, cos_w, sin_w, first_w).T.astype(qit_ref.dtype)

    q_a = _dot(c_q, wuq_ref[...])
    sq = q_a * q_a
    sq_hi = sq.astype(hmean_ref.dtype)
    sq_lo = (sq - sq_hi.astype(jnp.float32)).astype(hmean_ref.dtype)
    ms_q = _dot(sq_hi, hmean_ref[...]) + _dot(sq_lo, hmean_ref[...])
    q_a = q_a * lax.rsqrt(ms_q + EPS) * gqa_ref[...]
    qat_ref[...] = (_rope(q_a, cos_w, sin_w, first_w) * (HEAD_DIM ** -0.5)).T.astype(qat_ref.dtype)


def _proj(x1, pos, g_mix, w_a, w_w, w_b, g_cq, w_uq, w_qi, g_qa, g_ka, tm):
    n, d = x1.shape
    dt = _MXU_DT
    inv_freq = ROPE_THETA ** (-np.arange(0, HEAD_DIM, 2, dtype=np.float64) / HEAD_DIM)
    invf = jnp.asarray(np.tile(inv_freq, LANES // (HEAD_DIM // 2)).astype(np.float32)).reshape(1, LANES)
    head_of = np.arange(A_WIDTH) // HEAD_DIM
    hmean = jnp.asarray((head_of[:, None] == head_of[None, :]).astype(np.float32) / HEAD_DIM, dt)
    gqa = jnp.tile(g_qa, A_HEADS).reshape(1, A_WIDTH)
    gka = jnp.tile(g_ka, LANES // HEAD_DIM).reshape(1, LANES)

    def full(a):
        return pl.BlockSpec(a.shape, lambda i: (0,) * a.ndim)

    def rows(w):
        return pl.BlockSpec((tm, w), lambda i: (i, 0))

    args = (x1, pos, g_mix.reshape(1, d), w_a, w_w, w_b, g_cq.reshape(1, A_Q_RANK), w_uq, w_qi,
            gqa, gka, invf, hmean)
    in_specs = [rows(d), rows(1)] + [full(a) for a in args[2:]]

    def cols(w):
        return pl.BlockSpec((w, tm), lambda i: (0, i))

    def tok(w, t):
        return jax.ShapeDtypeStruct((n, w), t)

    def keymajor(w, t=dt):
        return jax.ShapeDtypeStruct((w, n), t)

    return pl.pallas_call(
        _proj_kernel,
        grid=(n // tm,),
        in_specs=in_specs,
        out_specs=[cols(A_WIDTH), cols(A_WIDTH), rows(LANES), rows(LANES), cols(A_V_LATENT), cols(LANES),
                   rows(2 * B_WIDTH), cols(B_WIDTH)],
        out_shape=[keymajor(A_WIDTH), keymajor(A_WIDTH), tok(LANES, dt), tok(LANES, dt), keymajor(A_V_LATENT),
                   keymajor(LANES, jnp.float32), tok(2 * B_WIDTH, dt), keymajor(B_WIDTH)],
        compiler_params=_cparams(("parallel",)),
        name="proj",
    )(*args)


def _stack_heads_t(qt_ref, out_ref, tq):
    row = lax.broadcasted_iota(jnp.int32, (LANES, tq), 0)
    halves = (row < HEAD_DIM, row >= HEAD_DIM)
    for h in range(qt_ref.shape[0] // HEAD_DIM):
        pair = qt_ref[(h // 2) * LANES:(h // 2 + 1) * LANES, :]
        out_ref[:, h * tq:(h + 1) * tq] = jnp.where(halves[h % 2], pair, jnp.zeros_like(pair))


def _dsa_kernel(qit_ref, wt_ref, qat_ref, ki_ref, ka_ref, vt_ref, wuvt_ref, o_ref,
                sk_ref, qis_ref, qas_ref, acc_ref, st_ref, thr_ref, excess_ref, *, tq, top_k, max_search):
    i = pl.program_id(1)
    n_chunks = i + 1
    n_search = (n_chunks * tq + SEARCH_CHUNK - 1) // SEARCH_CHUNK
    t_col = i * tq + lax.broadcasted_iota(jnp.int32, (1, tq), 1)
    nh = A_HEADS

    def keys(c, width):
        return pl.ds(pl.multiple_of(c * width, width), width)

    _stack_heads_t(qit_ref, qis_ref, tq)
    _stack_heads_t(qat_ref, qas_ref, tq)

    def score_chunk(c):
        d = _dot(ki_ref[keys(c, tq), :], qis_ref[...])
        s = jnp.zeros((tq, tq), jnp.float32)
        for h in range(IDX_HEADS):
            s = s + wt_ref[h:h + 1, :] * jnp.maximum(d[:, h * tq:(h + 1) * tq], 0.0)
        s = jnp.where(jnp.abs(s) < MIN_NORMAL, 0.0, s)
        bits = lax.bitcast_convert_type(s, jnp.int32)
        key = bits ^ ((bits >> 31) & 0x7FFFFFFF)
        s_pos = c * tq + lax.broadcasted_iota(jnp.int32, (tq, 1), 0)
        causal = s_pos <= t_col
        sk_ref[keys(c, tq), :] = jnp.where(causal, key, INT_MIN)
        top = lax.bitcast_convert_type(bits & TOP16, jnp.float32)
        st_ref[keys(c, tq), :] = jnp.where(causal, top, -jnp.inf).astype(st_ref.dtype)

    def score_body(c2, carry):
        score_chunk(2 * c2)
        score_chunk(2 * c2 + 1)
        return carry

    lax.fori_loop(0, n_search, score_body, 0)

    def tree_sum(terms):
        while len(terms) > 1:
            terms = [a + b for a, b in zip(terms[0::2], terms[1::2])] + terms[len(terms) & ~1:]
        return terms[0]

    def chunk_count(rows, thr):
        ge = jnp.where(sk_ref[rows, :] >= thr, 1.0, 0.0)
        ge = ge.reshape(SUBLANES, SEARCH_CHUNK // (SUBLANES * SUBLANES), SUBLANES, tq)
        return jnp.sum(jnp.sum(ge, axis=1), axis=0)

    def chunk_count_top(rows, thr_top):
        one, zero = jnp.ones((), st_ref.dtype), jnp.zeros((), st_ref.dtype)
        packed = 2 * SUBLANES
        ge = jnp.where(st_ref[rows, :] >= thr_top, one, zero).reshape(SEARCH_CHUNK // packed, packed, tq)
        return tree_sum([ge[r] for r in range(SEARCH_CHUNK // packed)]).astype(jnp.float32)

    def count_unrolled(chunk_fn, thr, n_static):
        part = tree_sum([chunk_fn(slice(c * SEARCH_CHUNK, (c + 1) * SEARCH_CHUNK), thr) for c in range(n_static)])
        return jnp.sum(part, axis=0, keepdims=True)

    def key_to_top(cand):
        k = cand ^ INT_MIN
        fbits = k ^ ((k >> 31) & 0x7FFF0000)
        mag = fbits & 0x7FFFFFFF
        above = jnp.where(fbits < 0, 0, MIN_NORMAL_BITS)
        fbits = jnp.where(mag == 0, fbits, jnp.where(mag < MIN_NORMAL_BITS, above, fbits))
        return lax.bitcast_convert_type(fbits, jnp.float32).astype(st_ref.dtype)

    def search(n_static):
        def top_bit_body(b, t_u):
            cand = t_u | lax.shift_left(jnp.int32(1), 31 - b)
            return jnp.where(count_unrolled(chunk_count_top, key_to_top(cand), n_static) >= top_k, cand, t_u)

        def bit_body(b, t_u):
            cand = t_u | lax.shift_left(jnp.int32(1), 31 - b)
            return jnp.where(count_unrolled(chunk_count, cand ^ INT_MIN, n_static) >= top_k, cand, t_u)

        t_u = lax.fori_loop(0, 16, top_bit_body, jnp.zeros((1, tq), jnp.int32))
        t_u = lax.fori_loop(16, 32, bit_body, t_u)
        thr_n = jnp.maximum(t_u ^ INT_MIN, INT_MIN + 1)
        thr_ref[...] = jnp.broadcast_to(thr_n, thr_ref.shape)
        excess_ref[...] = jnp.broadcast_to(count_unrolled(chunk_count, thr_n, n_static) - top_k, excess_ref.shape)

    lax.switch(n_search - 1, [functools.partial(search, n) for n in range(1, max_search + 1)])
    thr = thr_ref[0:1, :]

    excess = excess_ref[0:1, :]

    @pl.when(jnp.max(excess) > 0.0)
    def _():
        def ties_before(limit):
            def body(c, part):
                s_pos = c * SEARCH_CHUNK + lax.broadcasted_iota(jnp.int32, (SEARCH_CHUNK, 1), 0)
                hit = jnp.where(sk_ref[keys(c, SEARCH_CHUNK), :] == thr, 1.0, 0.0)
                hit = jnp.where(s_pos < limit, hit, 0.0)
                hit = hit.reshape(SUBLANES, SEARCH_CHUNK // (SUBLANES * SUBLANES), SUBLANES, tq)
                return part + jnp.sum(jnp.sum(hit, axis=1), axis=0)
            part = lax.fori_loop(0, n_search, body, jnp.zeros((SUBLANES, tq), jnp.float32))
            return jnp.sum(part, axis=0, keepdims=True)

        n_ties = ties_before(jnp.full((1, tq), n_search * SEARCH_CHUNK, jnp.int32))
        keep = n_ties - excess
        pos_bits = sk_ref.shape[0].bit_length()

        def pos_body(b, limit):
            cand = limit | lax.shift_left(jnp.int32(1), pos_bits - 1 - b)
            return jnp.where(ties_before(cand) <= keep, cand, limit)

        limit = lax.fori_loop(0, pos_bits, pos_body, jnp.zeros((1, tq), jnp.int32))

        def demote_body(c, carry):
            s_pos = c * SEARCH_CHUNK + lax.broadcasted_iota(jnp.int32, (SEARCH_CHUNK, 1), 0)
            key = sk_ref[keys(c, SEARCH_CHUNK), :]
            sk_ref[keys(c, SEARCH_CHUNK), :] = jnp.where(key == thr, jnp.where(s_pos >= limit, INT_MIN, key), key)
            return carry

        lax.fori_loop(0, n_search, demote_body, 0)

    acc_ref[...] = jnp.zeros_like(acc_ref)

    def attn_body(c, carry):
        m_run, l_run = carry
        lg = _dot(ka_ref[keys(c, tq), :], qas_ref[...])
        bias = jnp.where(sk_ref[keys(c, tq), :] >= thr, 0.0, NEG_BIG)
        x = lg + jnp.concatenate([bias] * nh, axis=1)
        m_new = jnp.maximum(m_run, jnp.max(x, axis=0, keepdims=True))
        alpha = jnp.exp(m_run - m_new)
        p = jnp.exp(x - m_new)
        l_new = alpha * l_run + jnp.sum(p, axis=0, keepdims=True)
        acc_ref[...] = alpha * acc_ref[...] + _dot(vt_ref[:, keys(c, tq)], p.astype(vt_ref.dtype))
        return m_new, l_new

    init = (jnp.full((1, nh * tq), NEG_BIG, jnp.float32), jnp.zeros((1, nh * tq), jnp.float32))
    _, l_fin = lax.fori_loop(0, n_chunks, attn_body, init)

    o_lat = (acc_ref[...] / l_fin).astype(wuvt_ref.dtype)
    outs = [_dot(wuvt_ref[h], o_lat[:, h * tq:(h + 1) * tq]) for h in range(nh)]
    o_ref[0] = jnp.concatenate(outs, axis=0).T.astype(o_ref.dtype)


def _dsa(qit, wit, qat, ki, ka, vat, w_uv_t, bsz, top_k, tq):
    n = qit.shape[1]
    s = n // bsz
    nq = s // tq
    assert s % SEARCH_CHUNK == 0 and SEARCH_CHUNK == 2 * tq
    dt = qit.dtype

    def qcols(w):
        return pl.BlockSpec((w, tq), lambda bi, i: (0, bi * nq + i))

    return pl.pallas_call(
        functools.partial(_dsa_kernel, tq=tq, top_k=top_k, max_search=s // SEARCH_CHUNK),
        grid=(bsz, nq),
        in_specs=[
            qcols(A_WIDTH), qcols(LANES), qcols(A_WIDTH),
            pl.BlockSpec((s, LANES), lambda bi, i: (bi, 0)),
            pl.BlockSpec((s, LANES), lambda bi, i: (bi, 0)),
            pl.BlockSpec((A_V_LATENT, s), lambda bi, i: (0, bi)),
            pl.BlockSpec((A_HEADS, HEAD_DIM, A_V_LATENT), lambda bi, i: (0, 0, 0)),
        ],
        out_specs=pl.BlockSpec((1, tq, A_WIDTH), lambda bi, i: (bi, i, 0)),
        out_shape=jax.ShapeDtypeStruct((bsz, s, A_WIDTH), dt),
        scratch_shapes=[pltpu.VMEM((s, tq), jnp.int32),
                        pltpu.VMEM((LANES, IDX_HEADS * tq), dt),
                        pltpu.VMEM((LANES, A_HEADS * tq), dt),
                        pltpu.VMEM((A_V_LATENT, A_HEADS * tq), jnp.float32),
                        pltpu.VMEM((s, tq), jnp.bfloat16),
                        pltpu.VMEM((SUBLANES, tq), jnp.int32),
                        pltpu.VMEM((SUBLANES, tq), jnp.float32)],
        compiler_params=_cparams(("parallel", "arbitrary")),
        name="dsa",
    )(qit, wit, qat, ki, ka, vat, w_uv_t)


def _stack_heads(q_ref, out_ref, tq):
    lane = lax.broadcasted_iota(jnp.int32, (tq, LANES), 1)
    halves = (lane < HEAD_DIM, lane >= HEAD_DIM)
    for h in range(q_ref.shape[-1] // HEAD_DIM):
        pair = q_ref[0, :, (h // 2) * LANES:(h // 2 + 1) * LANES]
        out_ref[h * tq:(h + 1) * tq, :] = jnp.where(halves[h % 2], pair, jnp.zeros_like(pair))


def _sb_kernel(q_ref, kt_ref, v_ref, o_ref, qs_ref, later_ref, acc_ref, *, tb):
    i = pl.program_id(1)
    row = lax.broadcasted_iota(jnp.int32, (tb, tb), 0)
    col = lax.broadcasted_iota(jnp.int32, (tb, tb), 1)
    later_ref[...] = jnp.where(row >= col, -1.0, 0.0).astype(later_ref.dtype)
    strict = col < row
    _stack_heads(q_ref, qs_ref, tb)
    acc_ref[...] = jnp.zeros_like(acc_ref)

    def block(j, carries, diag):
        ks = pl.ds(pl.multiple_of(j * tb, tb), tb)
        later = later_ref[...]
        zs, incls, out = {}, {}, [None] * B_HEADS

        def pair(h):
            return slice((h // 2) * LANES, (h // 2 + 1) * LANES)

        def logits(h):
            zs[h] = _dot(qs_ref[h * tb:(h + 1) * tb, :], kt_ref[pair(h), ks])

        def cumsum(h):
            sp = jnp.maximum(zs[h], 0.0) + jnp.log(1.0 + jnp.exp2(jnp.abs(zs[h]) * (-LOG2E)))
            if diag:
                sp = jnp.where(strict, sp, 0.0)
            incls[h] = _dot(sp.astype(later.dtype), later)

        def weights(h):
            a = jnp.exp2((zs[h] + incls[h] + carries[h]) * LOG2E)
            if diag:
                a = jnp.where(strict, a, 0.0)
            acc_ref[h] += _dot(a.astype(v_ref.dtype), v_ref[0, ks, pair(h)])
            out[h] = carries[h] + incls[h][:, 0:1]

        for step in range(B_HEADS + 2):
            if step < B_HEADS:
                logits(step)
            if 0 <= step - 1 < B_HEADS:
                cumsum(step - 1)
            if 0 <= step - 2 < B_HEADS:
                weights(step - 2)
        return tuple(out)

    carries = block(i, tuple(jnp.zeros((tb, 1), jnp.float32) for _ in range(B_HEADS)), True)
    lax.fori_loop(0, i, lambda k, c: block(i - 1 - k, c, False), carries)

    lane = lax.broadcasted_iota(jnp.int32, (tb, LANES), 1)
    for p in range(B_HEADS // 2):
        o_ref[0, :, p * LANES:(p + 1) * LANES] = jnp.where(
            lane < HEAD_DIM, acc_ref[2 * p], acc_ref[2 * p + 1]).astype(o_ref.dtype)


def _stickbreak(qv, kbt, tb):
    b, s, _ = qv.shape
    dt = qv.dtype
    return pl.pallas_call(
        functools.partial(_sb_kernel, tb=tb),
        grid=(b, s // tb),
        in_specs=[
            pl.BlockSpec((1, tb, B_WIDTH), lambda bi, i: (bi, i, 0)),
            pl.BlockSpec((B_WIDTH, s), lambda bi, i: (0, bi)),
            pl.BlockSpec((1, s, B_WIDTH), lambda bi, i: (bi, 0, 1)),
        ],
        out_specs=pl.BlockSpec((1, tb, B_WIDTH), lambda bi, i: (bi, i, 0)),
        out_shape=jax.ShapeDtypeStruct((b, s, B_WIDTH), dt),
        scratch_shapes=[pltpu.VMEM((B_HEADS * tb, LANES), dt),
                        pltpu.VMEM((tb, tb), dt),
                        pltpu.VMEM((B_HEADS, tb, LANES), jnp.float32)],
        compiler_params=_cparams(("parallel", "arbitrary")),
        name="stickbreak",
    )(qv, kbt, qv)


def _merge_kernel(x_ref, ya_ref, yb_ref, gmix_ref, wg_ref, woa_ref, wob_ref, wout_ref, o_ref):
    x = x_ref[...]
    d = x.shape[-1]
    h = _rms(x, gmix_ref[...]).astype(wg_ref.dtype)
    gates = _dot(h, wg_ref[...])
    merged = (_sigmoid(gates[:, :d]) * _dot(ya_ref[...], woa_ref[...])
              + _sigmoid(gates[:, d:]) * _dot(yb_ref[...], wob_ref[...]))
    o_ref[...] = x + _dot(merged.astype(wout_ref.dtype), wout_ref[...])


def _merge(x1, ya, yb, g_mix, w_g, w_oa, w_ob, w_out, tm):
    n, d = x1.shape

    def full(a):
        return pl.BlockSpec(a.shape, lambda i: (0,) * a.ndim)

    def rows(w):
        return pl.BlockSpec((tm, w), lambda i: (i, 0))

    args = (x1, ya, yb, g_mix.reshape(1, d), w_g, w_oa, w_ob, w_out)
    return pl.pallas_call(
        _merge_kernel,
        grid=(n // tm,),
        in_specs=[rows(d), rows(A_WIDTH), rows(B_WIDTH)] + [full(a) for a in args[3:]],
        out_specs=rows(d),
        out_shape=jax.ShapeDtypeStruct((n, d), jnp.float32),
        compiler_params=_cparams(("parallel",)),
        name="merge",
    )(*args)


def _pick(n, prefs):
    for p in prefs:
        if n % p == 0:
            return p
    return n


def _layer(x, positions, g_ffn1, w1_gate, w1_up, w1_down, g_mix, w_in, g_cq, w_uq_a, w_q_idx,
           g_q_a, g_k_a, w_uv_a, w_o_a, w_o_b, w_out, g_ffn2, w2_gate, w2_up, w2_down):
    bsz, seq, d = x.shape
    n = bsz * seq
    dt = _MXU_DT
    top_k = min(MAX_TOPK, seq // 4)
    tm_ffn = _pick(n, (2048, 1024, 512, 256))
    tf = _pick(w1_gate.shape[1], (256, 128))
    tm = _pick(n, (512, 256))
    tq = _pick(seq, (256, 128))
    tb = _pick(seq, (256, 128))

    xf = x.reshape(n, d)
    x1 = _ffn(xf, g_ffn1, w1_gate.astype(dt), w1_up.astype(dt), (0.5 * w1_down).astype(dt), tm_ffn, tf)

    o = np.cumsum((A_Q_RANK, HEAD_DIM, A_V_LATENT, IDX_DIM, IDX_HEADS, B_WIDTH, B_WIDTH, B_WIDTH, d, d))
    w_a = jnp.concatenate([w_in[:, :o[0]], w_in[:, o[0]:o[1]], w_in[:, o[2]:o[3]], w_in[:, o[1]:o[2]]],
                          axis=1).astype(dt)
    w_w = jnp.pad(w_in[:, o[3]:o[4]], ((0, 0), (0, LANES - IDX_HEADS))).astype(dt)
    w_b = jnp.concatenate([w_in[:, o[4]:o[5]] * (HEAD_DIM ** -0.5), w_in[:, o[5]:o[7]]], axis=1).astype(dt)
    w_g = w_in[:, o[7]:].astype(dt)

    qat, qit, ka, ki, vat, wit, qv, kbt = _proj(
        x1, positions.reshape(n, 1), g_mix, w_a, w_w, w_b, g_cq, w_uq_a.astype(dt), w_q_idx.astype(dt),
        g_q_a, g_k_a, tm)

    ya = _dsa(qit, wit, qat, ki, ka, vat, w_uv_a.transpose(0, 2, 1).astype(dt), bsz, top_k, tq)
    yb = _stickbreak(qv.reshape(bsz, seq, 2 * B_WIDTH), kbt, tb)

    x2 = _merge(x1, ya.reshape(n, A_WIDTH), yb.reshape(n, B_WIDTH), g_mix, w_g, w_o_a.astype(dt),
                w_o_b.astype(dt), w_out.astype(dt), tm)
    x3 = _ffn(x2, g_ffn2, w2_gate.astype(dt), w2_up.astype(dt), (0.5 * w2_down).astype(dt), tm_ffn, tf)
    return x3.reshape(bsz, seq, d)


def kernel(x, positions, g_ffn1, w1_gate, w1_up, w1_down, g_mix, w_in, g_cq, w_uq_a, w_q_idx, g_q_a, g_k_a,
           w_uv_a, w_o_a, w_o_b, w_out, g_ffn2, w2_gate, w2_up, w2_down):
    depth = g_ffn1.shape[0]
    for l in range(depth):
        x = _layer(x, positions, g_ffn1[l], w1_gate[l], w1_up[l], w1_down[l], g_mix[l], w_in[l], g_cq[l],
                   w_uq_a[l], w_q_idx[l], g_q_a[l], g_k_a[l], w_uv_a[l], w_o_a[l], w_o_b[l], w_out[l],
                   g_ffn2[l], w2_gate[l], w2_up[l], w2_down[l])
    return x
```

```python
import functools

import numpy as np
import jax
import jax.numpy as jnp
from jax import lax
from jax.experimental import pallas as pl
from jax.experimental.pallas import tpu as pltpu

HEAD_DIM = 64
A_HEADS = 8
A_Q_RANK = 256
A_V_LATENT = 128
V_ROWS = A_V_LATENT + 16
IDX_HEADS = 8
IDX_DIM = 64
B_HEADS = 8
MAX_TOPK = 256
ROPE_THETA = 10000.0
EPS = 1e-6
A_WIDTH = A_HEADS * HEAD_DIM
B_WIDTH = B_HEADS * HEAD_DIM
LANES = 128
SUBLANES = 8
INT_MIN = -(2 ** 31)
NEG_BIG = -1e30
LOG2E = 1.4426950408889634
SEARCH_CHUNK = 512

_MXU_DT = jnp.bfloat16
_VMEM_LIMIT = 56 * 1024 * 1024
TOP16 = -65536
MIN_NORMAL_BITS = 0x00800000
MIN_NORMAL = float(np.float32(2.0) ** -126)


def _cparams(sem):
    return pltpu.CompilerParams(dimension_semantics=sem, vmem_limit_bytes=_VMEM_LIMIT)


def _dot(a, b):
    return jnp.dot(a, b, preferred_element_type=jnp.float32)


def _sigmoid(x):
    return 1.0 / (1.0 + jnp.exp(-x))


def _rms(x, g):
    ms = jnp.mean(x * x, axis=-1, keepdims=True)
    return x * lax.rsqrt(ms + EPS) * g


def _ffn_kernel(x_ref, g_ref, wg_ref, wu_ref, wd_ref, o_ref, h_ref):
    @pl.when(pl.program_id(1) == 0)
    def _():
        x = x_ref[...]
        h_ref[...] = _rms(x, g_ref[...]).astype(h_ref.dtype)
        o_ref[...] = x

    h = h_ref[...]
    gate = _dot(h, wg_ref[...])
    up = _dot(h, wu_ref[...])
    act = (gate * _sigmoid(gate) * up).astype(wd_ref.dtype)
    o_ref[...] += _dot(act, wd_ref[...])


def _ffn(x, g, wg, wu, wd, tm, tf):
    n, d = x.shape
    dff = wg.shape[1]
    return pl.pallas_call(
        _ffn_kernel,
        grid=(n // tm, dff // tf),
        in_specs=[
            pl.BlockSpec((tm, d), lambda i, f: (i, 0)),
            pl.BlockSpec((1, d), lambda i, f: (0, 0)),
            pl.BlockSpec((d, tf), lambda i, f: (0, f)),
            pl.BlockSpec((d, tf), lambda i, f: (0, f)),
            pl.BlockSpec((tf, d), lambda i, f: (f, 0)),
        ],
        out_specs=pl.BlockSpec((tm, d), lambda i, f: (i, 0)),
        out_shape=jax.ShapeDtypeStruct((n, d), jnp.float32),
        scratch_shapes=[pltpu.VMEM((tm, d), _MXU_DT)],
        compiler_params=_cparams(("parallel", "arbitrary")),
        name="ffn",
    )(x, g.reshape(1, d), wg, wu, wd)


def _rope(x, cos, sin_signed, first_half):
    n = x.shape[-1]
    half = HEAD_DIM // 2
    fwd = pltpu.roll(x, n - half, 1)
    bwd = pltpu.roll(x, half, 1)
    return x * cos + jnp.where(first_half, fwd, bwd) * sin_signed


def _proj_kernel(x_ref, pos_ref, gmix_ref, wa_ref, ww_ref, wb_ref, gcq_ref, wuq_ref, wqi_ref,
                 gqa_ref, gka_ref, invf_ref, hmean_ref,
                 qat_ref, qit_ref, ka_ref, ki_ref, vat_ref, wit_ref, qv_ref, kbt_ref):
    h = _rms(x_ref[...], gmix_ref[...]).astype(wa_ref.dtype)

    r = _dot(h, wb_ref[...])
    qv_ref[:, :B_WIDTH] = r[:, :B_WIDTH].astype(qv_ref.dtype)
    qv_ref[:, B_WIDTH:] = r[:, 2 * B_WIDTH:].astype(qv_ref.dtype)
    kbt_ref[...] = r[:, B_WIDTH:2 * B_WIDTH].T.astype(kbt_ref.dtype)
    wit_ref[...] = (_dot(h, ww_ref[...]) * ((IDX_HEADS ** -0.5) * (IDX_DIM ** -0.5))).T

    pa = _dot(h, wa_ref[...])
    c_q = _rms(pa[:, :A_Q_RANK], gcq_ref[...]).astype(wuq_ref.dtype)
    kk = pa[:, A_Q_RANK:A_Q_RANK + LANES]
    vat_ref[:A_V_LATENT, :] = pa[:, A_Q_RANK + LANES:].T.astype(vat_ref.dtype)
    vat_ref[A_V_LATENT:, :] = jnp.ones((V_ROWS - A_V_LATENT, vat_ref.shape[1]), vat_ref.dtype)

    ang = pos_ref[...].astype(jnp.float32) * invf_ref[...]
    cos = jnp.cos(ang)
    sin = jnp.sin(ang)
    lane = lax.broadcasted_iota(jnp.int32, cos.shape, 1)
    first_half = (lane & (HEAD_DIM // 2)) == 0
    sin_signed = jnp.where(first_half, -sin, sin)
    reps = A_WIDTH // LANES
    cos_w = jnp.concatenate([cos] * reps, axis=1)
    sin_w = jnp.concatenate([sin_signed] * reps, axis=1)
    first_w = (lax.broadcasted_iota(jnp.int32, cos_w.shape, 1) & (HEAD_DIM // 2)) == 0

    is_ka = lane < HEAD_DIM
    ms_k = jnp.sum(jnp.where(is_ka, kk * kk, 0.0), axis=-1, keepdims=True) * (1.0 / HEAD_DIM)
    kscale = jnp.where(is_ka, lax.rsqrt(ms_k + EPS) * gka_ref[...], 1.0)
    kk = _rope(kk * kscale, cos, sin_signed, first_half)
    swapped = pltpu.roll(kk, HEAD_DIM, 1)
    ka_ref[...] = jnp.where(is_ka, kk, swapped).astype(ka_ref.dtype)
    ki_ref[...] = jnp.where(is_ka, swapped, kk).astype(ki_ref.dtype)

    q_i = _dot(c_q, wqi_ref[...])
    qit_ref[...] = _rope(q_i, cos_w, sin_w, first_w).T.astype(qit_ref.dtype)

    q_a = _dot(c_q, wuq_ref[...])
    sq = q_a * q_a
    sq_hi = sq.astype(hmean_ref.dtype)
    sq_lo = (sq - sq_hi.astype(jnp.float32)).astype(hmean_ref.dtype)
    ms_q = _dot(sq_hi, hmean_ref[...]) + _dot(sq_lo, hmean_ref[...])
    q_a = q_a * lax.rsqrt(ms_q + EPS) * gqa_ref[...]
    qat_ref[...] = (_rope(q_a, cos_w, sin_w, first_w) * (HEAD_DIM ** -0.5)).T.astype(qat_ref.dtype)


def _proj(x1, pos, g_mix, w_a, w_w, w_b, g_cq, w_uq, w_qi, g_qa, g_ka, tm):
    n, d = x1.shape
    dt = _MXU_DT
    inv_freq = ROPE_THETA ** (-np.arange(0, HEAD_DIM, 2, dtype=np.float64) / HEAD_DIM)
    invf = jnp.asarray(np.tile(inv_freq, LANES // (HEAD_DIM // 2)).astype(np.float32)).reshape(1, LANES)
    head_of = np.arange(A_WIDTH) // HEAD_DIM
    hmean = jnp.asarray((head_of[:, None] == head_of[None, :]).astype(np.float32) / HEAD_DIM, dt)
    gqa = jnp.tile(g_qa, A_HEADS).reshape(1, A_WIDTH)
    gka = jnp.tile(g_ka, LANES // HEAD_DIM).reshape(1, LANES)

    def full(a):
        return pl.BlockSpec(a.shape, lambda i: (0,) * a.ndim)

    def rows(w):
        return pl.BlockSpec((tm, w), lambda i: (i, 0))

    args = (x1, pos, g_mix.reshape(1, d), w_a, w_w, w_b, g_cq.reshape(1, A_Q_RANK), w_uq, w_qi,
            gqa, gka, invf, hmean)
    in_specs = [rows(d), rows(1)] + [full(a) for a in args[2:]]

    def cols(w):
        return pl.BlockSpec((w, tm), lambda i: (0, i))

    def tok(w, t):
        return jax.ShapeDtypeStruct((n, w), t)

    def keymajor(w, t=dt):
        return jax.ShapeDtypeStruct((w, n), t)

    return pl.pallas_call(
        _proj_kernel,
        grid=(n // tm,),
        in_specs=in_specs,
        out_specs=[cols(A_WIDTH), cols(A_WIDTH), rows(LANES), rows(LANES), cols(V_ROWS), cols(LANES),
                   rows(2 * B_WIDTH), cols(B_WIDTH)],
        out_shape=[keymajor(A_WIDTH), keymajor(A_WIDTH), tok(LANES, dt), tok(LANES, dt), keymajor(V_ROWS),
                   keymajor(LANES, jnp.float32), tok(2 * B_WIDTH, dt), keymajor(B_WIDTH)],
        compiler_params=_cparams(("parallel",)),
        name="proj",
    )(*args)


def _stack_heads_t(qt_ref, out_ref, tq):
    row = lax.broadcasted_iota(jnp.int32, (LANES, tq), 0)
    halves = (row < HEAD_DIM, row >= HEAD_DIM)
    for h in range(qt_ref.shape[0] // HEAD_DIM):
        pair = qt_ref[(h // 2) * LANES:(h // 2 + 1) * LANES, :]
        out_ref[:, h * tq:(h + 1) * tq] = jnp.where(halves[h % 2], pair, jnp.zeros_like(pair))


def _dsa_kernel(qit_ref, wt_ref, qat_ref, ki_ref, ka_ref, vt_ref, wuvt_ref, o_ref,
                sk_ref, qis_ref, qas_ref, acc_ref, st_ref, thr_ref, excess_ref, *, tq, top_k, max_search):
    i = pl.program_id(1)
    n_chunks = i + 1
    n_search = (n_chunks * tq + SEARCH_CHUNK - 1) // SEARCH_CHUNK
    t_col = i * tq + lax.broadcasted_iota(jnp.int32, (1, tq), 1)
    nh = A_HEADS

    def keys(c, width):
        return pl.ds(pl.multiple_of(c * width, width), width)

    _stack_heads_t(qit_ref, qis_ref, tq)
    _stack_heads_t(qat_ref, qas_ref, tq)

    def score_chunk(c):
        d = _dot(ki_ref[keys(c, tq), :], qis_ref[...])
        s = jnp.zeros((tq, tq), jnp.float32)
        for h in range(IDX_HEADS):
            s = s + wt_ref[h:h + 1, :] * jnp.maximum(d[:, h * tq:(h + 1) * tq], 0.0)
        s = jnp.where(jnp.abs(s) < MIN_NORMAL, 0.0, s)
        bits = lax.bitcast_convert_type(s, jnp.int32)
        key = bits ^ ((bits >> 31) & 0x7FFFFFFF)
        s_pos = c * tq + lax.broadcasted_iota(jnp.int32, (tq, 1), 0)
        causal = s_pos <= t_col
        sk_ref[keys(c, tq), :] = jnp.where(causal, key, INT_MIN)
        top = lax.bitcast_convert_type(bits & TOP16, jnp.float32)
        st_ref[keys(c, tq), :] = jnp.where(causal, top, -jnp.inf).astype(st_ref.dtype)

    def score_body(c2, carry):
        score_chunk(2 * c2)
        score_chunk(2 * c2 + 1)
        return carry

    lax.fori_loop(0, n_search, score_body, 0)

    def tree_sum(terms):
        while len(terms) > 1:
            terms = [a + b for a, b in zip(terms[0::2], terms[1::2])] + terms[len(terms) & ~1:]
        return terms[0]

    def chunk_count(rows, thr):
        ge = jnp.where(sk_ref[rows, :] >= thr, 1.0, 0.0)
        ge = ge.reshape(SUBLANES, SEARCH_CHUNK // (SUBLANES * SUBLANES), SUBLANES, tq)
        return jnp.sum(jnp.sum(ge, axis=1), axis=0)

    def chunk_count_top(rows, thr_top):
        one, zero = jnp.ones((), st_ref.dtype), jnp.zeros((), st_ref.dtype)
        packed = 2 * SUBLANES
        ge = jnp.where(st_ref[rows, :] >= thr_top, one, zero).reshape(SEARCH_CHUNK // packed, packed, tq)
        return tree_sum([ge[r] for r in range(SEARCH_CHUNK // packed)]).astype(jnp.float32)

    def count_unrolled(chunk_fn, thr, n_static):
        part = tree_sum([chunk_fn(slice(c * SEARCH_CHUNK, (c + 1) * SEARCH_CHUNK), thr) for c in range(n_static)])
        return jnp.sum(part, axis=0, keepdims=True)

    def key_to_top(cand):
        k = cand ^ INT_MIN
        fbits = k ^ ((k >> 31) & 0x7FFF0000)
        mag = fbits & 0x7FFFFFFF
        above = jnp.where(fbits < 0, 0, MIN_NORMAL_BITS)
        fbits = jnp.where(mag == 0, fbits, jnp.where(mag < MIN_NORMAL_BITS, above, fbits))
        return lax.bitcast_convert_type(fbits, jnp.float32).astype(st_ref.dtype)

    def search(n_static):
        def top_bit_body(b, t_u):
            cand = t_u | lax.shift_left(jnp.int32(1), 31 - b)
            return jnp.where(count_unrolled(chunk_count_top, key_to_top(cand), n_static) >= top_k, cand, t_u)

        def bit_body(b, t_u):
            cand = t_u | lax.shift_left(jnp.int32(1), 31 - b)
            return jnp.where(count_unrolled(chunk_count, cand ^ INT_MIN, n_static) >= top_k, cand, t_u)

        t_u = lax.fori_loop(0, 16, top_bit_body, jnp.zeros((1, tq), jnp.int32))
        t_u = lax.fori_loop(16, 32, bit_body, t_u)
        thr_n = jnp.maximum(t_u ^ INT_MIN, INT_MIN + 1)
        thr_ref[...] = jnp.broadcast_to(thr_n, thr_ref.shape)
        excess_ref[...] = jnp.broadcast_to(count_unrolled(chunk_count, thr_n, n_static) - top_k, excess_ref.shape)

    lax.switch(n_search - 1, [functools.partial(search, n) for n in range(1, max_search + 1)])
    thr = thr_ref[0:1, :]

    excess = excess_ref[0:1, :]

    @pl.when(jnp.max(excess) > 0.0)
    def _():
        def ties_before(limit):
            def body(c, part):
                s_pos = c * SEARCH_CHUNK + lax.broadcasted_iota(jnp.int32, (SEARCH_CHUNK, 1), 0)
                hit = jnp.where(sk_ref[keys(c, SEARCH_CHUNK), :] == thr, 1.0, 0.0)
                hit = jnp.where(s_pos < limit, hit, 0.0)
                hit = hit.reshape(SUBLANES, SEARCH_CHUNK // (SUBLANES * SUBLANES), SUBLANES, tq)
                return part + jnp.sum(jnp.sum(hit, axis=1), axis=0)
            part = lax.fori_loop(0, n_search, body, jnp.zeros((SUBLANES, tq), jnp.float32))
            return jnp.sum(part, axis=0, keepdims=True)

        n_ties = ties_before(jnp.full((1, tq), n_search * SEARCH_CHUNK, jnp.int32))
        keep = n_ties - excess
        pos_bits = sk_ref.shape[0].bit_length()

        def pos_body(b, limit):
            cand = limit | lax.shift_left(jnp.int32(1), pos_bits - 1 - b)
            return jnp.where(ties_before(cand) <= keep, cand, limit)

        limit = lax.fori_loop(0, pos_bits, pos_body, jnp.zeros((1, tq), jnp.int32))

        def demote_body(c, carry):
            s_pos = c * SEARCH_CHUNK + lax.broadcasted_iota(jnp.int32, (SEARCH_CHUNK, 1), 0)
            key = sk_ref[keys(c, SEARCH_CHUNK), :]
            sk_ref[keys(c, SEARCH_CHUNK), :] = jnp.where(key == thr, jnp.where(s_pos >= limit, INT_MIN, key), key)
            return carry

        lax.fori_loop(0, n_search, demote_body, 0)

    acc_ref[...] = jnp.zeros_like(acc_ref)

    def attn_body(c, m_run):
        lg = _dot(ka_ref[keys(c, tq), :], qas_ref[...])
        bias = jnp.where(sk_ref[keys(c, tq), :] >= thr, 0.0, NEG_BIG)
        x = lg + jnp.concatenate([bias] * nh, axis=1)
        m_new = jnp.maximum(m_run, jnp.max(x, axis=0, keepdims=True))
        alpha = jnp.exp(m_run - m_new)
        p = jnp.exp(x - m_new)
        acc_ref[...] = alpha * acc_ref[...] + _dot(vt_ref[:, keys(c, tq)], p.astype(vt_ref.dtype))
        return m_new

    lax.fori_loop(0, n_chunks, attn_body, jnp.full((1, nh * tq), NEG_BIG, jnp.float32))

    o_lat = (acc_ref[:A_V_LATENT, :] / acc_ref[A_V_LATENT:A_V_LATENT + 1, :]).astype(wuvt_ref.dtype)
    outs = [_dot(wuvt_ref[h], o_lat[:, h * tq:(h + 1) * tq]) for h in range(nh)]
    o_ref[0] = jnp.concatenate(outs, axis=0).T.astype(o_ref.dtype)


def _dsa(qit, wit, qat, ki, ka, vat, w_uv_t, bsz, top_k, tq):
    n = qit.shape[1]
    s = n // bsz
    nq = s // tq
    assert s % SEARCH_CHUNK == 0 and SEARCH_CHUNK == 2 * tq
    dt = qit.dtype

    def qcols(w):
        return pl.BlockSpec((w, tq), lambda bi, i: (0, bi * nq + i))

    return pl.pallas_call(
        functools.partial(_dsa_kernel, tq=tq, top_k=top_k, max_search=s // SEARCH_CHUNK),
        grid=(bsz, nq),
        in_specs=[
            qcols(A_WIDTH), qcols(LANES), qcols(A_WIDTH),
            pl.BlockSpec((s, LANES), lambda bi, i: (bi, 0)),
            pl.BlockSpec((s, LANES), lambda bi, i: (bi, 0)),
            pl.BlockSpec((V_ROWS, s), lambda bi, i: (0, bi)),
            pl.BlockSpec((A_HEADS, HEAD_DIM, A_V_LATENT), lambda bi, i: (0, 0, 0)),
        ],
        out_specs=pl.BlockSpec((1, tq, A_WIDTH), lambda bi, i: (bi, i, 0)),
        out_shape=jax.ShapeDtypeStruct((bsz, s, A_WIDTH), dt),
        scratch_shapes=[pltpu.VMEM((s, tq), jnp.int32),
                        pltpu.VMEM((LANES, IDX_HEADS * tq), dt),
                        pltpu.VMEM((LANES, A_HEADS * tq), dt),
                        pltpu.VMEM((V_ROWS, A_HEADS * tq), jnp.float32),
                        pltpu.VMEM((s, tq), jnp.bfloat16),
                        pltpu.VMEM((SUBLANES, tq), jnp.int32),
                        pltpu.VMEM((SUBLANES, tq), jnp.float32)],
        compiler_params=_cparams(("parallel", "arbitrary")),
        name="dsa",
    )(qit, wit, qat, ki, ka, vat, w_uv_t)


def _stack_heads(q_ref, out_ref, tq):
    lane = lax.broadcasted_iota(jnp.int32, (tq, LANES), 1)
    halves = (lane < HEAD_DIM, lane >= HEAD_DIM)
    for h in range(q_ref.shape[-1] // HEAD_DIM):
        pair = q_ref[0, :, (h // 2) * LANES:(h // 2 + 1) * LANES]
        out_ref[h * tq:(h + 1) * tq, :] = jnp.where(halves[h % 2], pair, jnp.zeros_like(pair))


def _sb_kernel(q_ref, kt_ref, v_ref, o_ref, qs_ref, later_ref, acc_ref, *, tb):
    i = pl.program_id(1)
    row = lax.broadcasted_iota(jnp.int32, (tb, tb), 0)
    col = lax.broadcasted_iota(jnp.int32, (tb, tb), 1)
    later_ref[...] = jnp.where(row >= col, -1.0, 0.0).astype(later_ref.dtype)
    strict = col < row
    _stack_heads(q_ref, qs_ref, tb)
    acc_ref[...] = jnp.zeros_like(acc_ref)

    def block(j, carries, diag):
        ks = pl.ds(pl.multiple_of(j * tb, tb), tb)
        later = later_ref[...]
        zs, incls, out = {}, {}, [None] * B_HEADS

        def pair(h):
            return slice((h // 2) * LANES, (h // 2 + 1) * LANES)

        def logits(h):
            zs[h] = _dot(qs_ref[h * tb:(h + 1) * tb, :], kt_ref[pair(h), ks])

        def cumsum(h):
            sp = jnp.maximum(zs[h], 0.0) + jnp.log(1.0 + jnp.exp2(jnp.abs(zs[h]) * (-LOG2E)))
            if diag:
                sp = jnp.where(strict, sp, 0.0)
            incls[h] = _dot(sp.astype(later.dtype), later)

        def weights(h):
            a = jnp.exp2((zs[h] + incls[h] + carries[h]) * LOG2E)
            if diag:
                a = jnp.where(strict, a, 0.0)
            acc_ref[h] += _dot(a.astype(v_ref.dtype), v_ref[0, ks, pair(h)])
            out[h] = carries[h] + incls[h][:, 0:1]

        for step in range(B_HEADS + 2):
            if step < B_HEADS:
                logits(step)
            if 0 <= step - 1 < B_HEADS:
                cumsum(step - 1)
            if 0 <= step - 2 < B_HEADS:
                weights(step - 2)
        return tuple(out)

    carries = block(i, tuple(jnp.zeros((tb, 1), jnp.float32) for _ in range(B_HEADS)), True)
    lax.fori_loop(0, i, lambda k, c: block(i - 1 - k, c, False), carries)

    lane = lax.broadcasted_iota(jnp.int32, (tb, LANES), 1)
    for p in range(B_HEADS // 2):
        o_ref[0, :, p * LANES:(p + 1) * LANES] = jnp.where(
            lane < HEAD_DIM, acc_ref[2 * p], acc_ref[2 * p + 1]).astype(o_ref.dtype)


def _stickbreak(qv, kbt, tb):
    b, s, _ = qv.shape
    dt = qv.dtype
    return pl.pallas_call(
        functools.partial(_sb_kernel, tb=tb),
        grid=(b, s // tb),
        in_specs=[
            pl.BlockSpec((1, tb, B_WIDTH), lambda bi, i: (bi, i, 0)),
            pl.BlockSpec((B_WIDTH, s), lambda bi, i: (0, bi)),
            pl.BlockSpec((1, s, B_WIDTH), lambda bi, i: (bi, 0, 1)),
        ],
        out_specs=pl.BlockSpec((1, tb, B_WIDTH), lambda bi, i: (bi, i, 0)),
        out_shape=jax.ShapeDtypeStruct((b, s, B_WIDTH), dt),
        scratch_shapes=[pltpu.VMEM((B_HEADS * tb, LANES), dt),
                        pltpu.VMEM((tb, tb), dt),
                        pltpu.VMEM((B_HEADS, tb, LANES), jnp.float32)],
        compiler_params=_cparams(("parallel", "arbitrary")),
        name="stickbreak",
    )(qv, kbt, qv)


def _merge_kernel(x_ref, ya_ref, yb_ref, gmix_ref, wg_ref, woa_ref, wob_ref, wout_ref, o_ref):
    x = x_ref[...]
    d = x.shape[-1]
    h = _rms(x, gmix_ref[...]).astype(wg_ref.dtype)
    gates = _dot(h, wg_ref[...])
    merged = (_sigmoid(gates[:, :d]) * _dot(ya_ref[...], woa_ref[...])
              + _sigmoid(gates[:, d:]) * _dot(yb_ref[...], wob_ref[...]))
    o_ref[...] = x + _dot(merged.astype(wout_ref.dtype), wout_ref[...])


def _merge(x1, ya, yb, g_mix, w_g, w_oa, w_ob, w_out, tm):
    n, d = x1.shape

    def full(a):
        return pl.BlockSpec(a.shape, lambda i: (0,) * a.ndim)

    def rows(w):
        return pl.BlockSpec((tm, w), lambda i: (i, 0))

    args = (x1, ya, yb, g_mix.reshape(1, d), w_g, w_oa, w_ob, w_out)
    return pl.pallas_call(
        _merge_kernel,
        grid=(n // tm,),
        in_specs=[rows(d), rows(A_WIDTH), rows(B_WIDTH)] + [full(a) for a in args[3:]],
        out_specs=rows(d),
        out_shape=jax.ShapeDtypeStruct((n, d), jnp.float32),
        compiler_params=_cparams(("parallel",)),
        name="merge",
    )(*args)


def _pick(n, prefs):
    for p in prefs:
        if n % p == 0:
            return p
    return n


def _layer(x, positions, g_ffn1, w1_gate, w1_up, w1_down, g_mix, w_in, g_cq, w_uq_a, w_q_idx,
           g_q_a, g_k_a, w_uv_a, w_o_a, w_o_b, w_out, g_ffn2, w2_gate, w2_up, w2_down):
    bsz, seq, d = x.shape
    n = bsz * seq
    dt = _MXU_DT
    top_k = min(MAX_TOPK, seq // 4)
    tm_ffn = _pick(n, (2048, 1024, 512, 256))
    tf = _pick(w1_gate.shape[1], (256, 128))
    tm = _pick(n, (1024, 512, 256))
    tq = _pick(seq, (256, 128))
    tb = _pick(seq, (256, 128))

    xf = x.reshape(n, d)
    x1 = _ffn(xf, g_ffn1, w1_gate.astype(dt), w1_up.astype(dt), (0.5 * w1_down).astype(dt), tm_ffn, tf)

    o = np.cumsum((A_Q_RANK, HEAD_DIM, A_V_LATENT, IDX_DIM, IDX_HEADS, B_WIDTH, B_WIDTH, B_WIDTH, d, d))
    w_a = jnp.concatenate([w_in[:, :o[0]], w_in[:, o[0]:o[1]], w_in[:, o[2]:o[3]], w_in[:, o[1]:o[2]]],
                          axis=1).astype(dt)
    w_w = jnp.pad(w_in[:, o[3]:o[4]], ((0, 0), (0, LANES - IDX_HEADS))).astype(dt)
    w_b = jnp.concatenate([w_in[:, o[4]:o[5]] * (HEAD_DIM ** -0.5), w_in[:, o[5]:o[7]]], axis=1).astype(dt)
    w_g = w_in[:, o[7]:].astype(dt)

    qat, qit, ka, ki, vat, wit, qv, kbt = _proj(
        x1, positions.reshape(n, 1), g_mix, w_a, w_w, w_b, g_cq, w_uq_a.astype(dt), w_q_idx.astype(dt),
        g_q_a, g_k_a, tm)

    ya = _dsa(qit, wit, qat, ki, ka, vat, w_uv_a.transpose(0, 2, 1).astype(dt), bsz, top_k, tq)
    yb = _stickbreak(qv.reshape(bsz, seq, 2 * B_WIDTH), kbt, tb)

    x2 = _merge(x1, ya.reshape(n, A_WIDTH), yb.reshape(n, B_WIDTH), g_mix, w_g, w_o_a.astype(dt),
                w_o_b.astype(dt), w_out.astype(dt), tm)
    x3 = _ffn(x2, g_ffn2, w2_gate.astype(dt), w2_up.astype(dt), (0.5 * w2_down).astype(dt), tm_ffn, tf)
    return x3.reshape(bsz, seq, d)


def kernel(x, positions, g_ffn1, w1_gate, w1_up, w1_down, g_mix, w_in, g_cq, w_uq_a, w_q_idx, g_q_a, g_k_a,
           w_uv_a, w_o_a, w_o_b, w_out, g_ffn2, w2_gate, w2_up, w2_down):
    depth = g_ffn1.shape[0]
    for l in range(depth):
        x = _layer(x, positions, g_ffn1[l], w1_gate[l], w1_up[l], w1_down[l], g_mix[l], w_in[l], g_cq[l],
                   w_uq_a[l], w_q_idx[l], g_q_a[l], g_k_a[l], w_uv_a[l], w_o_a[l], w_o_b[l], w_out[l],
                   g_ffn2[l], w2_gate[l], w2_up[l], w2_down[l])
    return x
```

```python
import functools

import numpy as np
import jax
import jax.numpy as jnp
from jax import lax
from jax.experimental import pallas as pl
from jax.experimental.pallas import tpu as pltpu

HEAD_DIM = 64
A_HEADS = 8
A_Q_RANK = 256
A_V_LATENT = 128
V_ROWS = A_V_LATENT + 16
IDX_HEADS = 8
IDX_DIM = 64
B_HEADS = 8
MAX_TOPK = 256
ROPE_THETA = 10000.0
EPS = 1e-6
A_WIDTH = A_HEADS * HEAD_DIM
B_WIDTH = B_HEADS * HEAD_DIM
LANES = 128
SUBLANES = 8
INT_MIN = -(2 ** 31)
NEG_BIG = -1e30
LOG2E = 1.4426950408889634
SEARCH_CHUNK = 512

_MXU_DT = jnp.bfloat16
_VMEM_LIMIT = 56 * 1024 * 1024
TOP16 = -65536
MIN_NORMAL_BITS = 0x00800000
MIN_NORMAL = float(np.float32(2.0) ** -126)


def _cparams(sem):
    return pltpu.CompilerParams(dimension_semantics=sem, vmem_limit_bytes=_VMEM_LIMIT)


def _dot(a, b):
    return jnp.dot(a, b, preferred_element_type=jnp.float32)


def _sigmoid(x):
    return 1.0 / (1.0 + jnp.exp(-x))


def _rms(x, g):
    ms = jnp.mean(x * x, axis=-1, keepdims=True)
    return x * lax.rsqrt(ms + EPS) * g


def _ffn_kernel(x_ref, g_ref, wg_ref, wu_ref, wd_ref, o_ref, h_ref):
    @pl.when(pl.program_id(1) == 0)
    def _():
        x = x_ref[...]
        h_ref[...] = _rms(x, g_ref[...]).astype(h_ref.dtype)
        o_ref[...] = x

    h = h_ref[...]
    gate = _dot(h, wg_ref[...])
    up = _dot(h, wu_ref[...])
    act = (gate * _sigmoid(gate) * up).astype(wd_ref.dtype)
    o_ref[...] += _dot(act, wd_ref[...])


def _ffn(x, g, wg, wu, wd, tm, tf):
    n, d = x.shape
    dff = wg.shape[1]
    return pl.pallas_call(
        _ffn_kernel,
        grid=(n // tm, dff // tf),
        in_specs=[
            pl.BlockSpec((tm, d), lambda i, f: (i, 0)),
            pl.BlockSpec((1, d), lambda i, f: (0, 0)),
            pl.BlockSpec((d, tf), lambda i, f: (0, f)),
            pl.BlockSpec((d, tf), lambda i, f: (0, f)),
            pl.BlockSpec((tf, d), lambda i, f: (f, 0)),
        ],
        out_specs=pl.BlockSpec((tm, d), lambda i, f: (i, 0)),
        out_shape=jax.ShapeDtypeStruct((n, d), jnp.float32),
        scratch_shapes=[pltpu.VMEM((tm, d), _MXU_DT)],
        compiler_params=_cparams(("parallel", "arbitrary")),
        name="ffn",
    )(x, g.reshape(1, d), wg, wu, wd)


def _rope(x, cos, sin_signed, first_half):
    n = x.shape[-1]
    half = HEAD_DIM // 2
    fwd = pltpu.roll(x, n - half, 1)
    bwd = pltpu.roll(x, half, 1)
    return x * cos + jnp.where(first_half, fwd, bwd) * sin_signed


def _proj_kernel(x_ref, pos_ref, gmix_ref, wa_ref, ww_ref, wb_ref, gcq_ref, wuq_ref, wqi_ref,
                 gqa_ref, gka_ref, invf_ref, hmean_ref,
                 qat_ref, qit_ref, ka_ref, ki_ref, vat_ref, wit_ref, qv_ref, kbt_ref):
    h = _rms(x_ref[...], gmix_ref[...]).astype(wa_ref.dtype)

    r = _dot(h, wb_ref[...])
    qv_ref[:, :B_WIDTH] = r[:, :B_WIDTH].astype(qv_ref.dtype)
    qv_ref[:, B_WIDTH:] = r[:, 2 * B_WIDTH:].astype(qv_ref.dtype)
    kbt_ref[...] = r[:, B_WIDTH:2 * B_WIDTH].T.astype(kbt_ref.dtype)
    wit_ref[...] = (_dot(h, ww_ref[...]) * ((IDX_HEADS ** -0.5) * (IDX_DIM ** -0.5))).T

    pa = _dot(h, wa_ref[...])
    c_q = _rms(pa[:, :A_Q_RANK], gcq_ref[...]).astype(wuq_ref.dtype)
    kk = pa[:, A_Q_RANK:A_Q_RANK + LANES]
    vat_ref[:A_V_LATENT, :] = pa[:, A_Q_RANK + LANES:].T.astype(vat_ref.dtype)
    vat_ref[A_V_LATENT:, :] = jnp.ones((V_ROWS - A_V_LATENT, vat_ref.shape[1]), vat_ref.dtype)

    ang = pos_ref[...].astype(jnp.float32) * invf_ref[...]
    cos = jnp.cos(ang)
    sin = jnp.sin(ang)
    lane = lax.broadcasted_iota(jnp.int32, cos.shape, 1)
    first_half = (lane & (HEAD_DIM // 2)) == 0
    sin_signed = jnp.where(first_half, -sin, sin)
    reps = A_WIDTH // LANES
    cos_w = jnp.concatenate([cos] * reps, axis=1)
    sin_w = jnp.concatenate([sin_signed] * reps, axis=1)
    first_w = (lax.broadcasted_iota(jnp.int32, cos_w.shape, 1) & (HEAD_DIM // 2)) == 0

    is_ka = lane < HEAD_DIM
    ms_k = jnp.sum(jnp.where(is_ka, kk * kk, 0.0), axis=-1, keepdims=True) * (1.0 / HEAD_DIM)
    kscale = jnp.where(is_ka, lax.rsqrt(ms_k + EPS) * gka_ref[...], 1.0)
    kk = _rope(kk * kscale, cos, sin_signed, first_half)
    swapped = pltpu.roll(kk, HEAD_DIM, 1)
    ka_ref[...] = jnp.where(is_ka, kk, swapped).astype(ka_ref.dtype)
    ki_ref[...] = jnp.where(is_ka, swapped, kk).astype(ki_ref.dtype)

    q_i = _dot(c_q, wqi_ref[...])
    qit_ref[...] = _rope(q_i, cos_w, sin_w, first_w).T.astype(qit_ref.dtype)

    q_a = _dot(c_q, wuq_ref[...])
    sq = q_a * q_a
    sq_hi = sq.astype(hmean_ref.dtype)
    sq_lo = (sq - sq_hi.astype(jnp.float32)).astype(hmean_ref.dtype)
    ms_q = _dot(sq_hi, hmean_ref[...]) + _dot(sq_lo, hmean_ref[...])
    q_a = q_a * lax.rsqrt(ms_q + EPS) * gqa_ref[...]
    qat_ref[...] = (_rope(q_a, cos_w, sin_w, first_w) * (HEAD_DIM ** -0.5)).T.astype(qat_ref.dtype)


def _proj(x1, pos, g_mix, w_a, w_w, w_b, g_cq, w_uq, w_qi, g_qa, g_ka, tm):
    n, d = x1.shape
    dt = _MXU_DT
    inv_freq = ROPE_THETA ** (-np.arange(0, HEAD_DIM, 2, dtype=np.float64) / HEAD_DIM)
    invf = jnp.asarray(np.tile(inv_freq, LANES // (HEAD_DIM // 2)).astype(np.float32)).reshape(1, LANES)
    head_of = np.arange(A_WIDTH) // HEAD_DIM
    hmean = jnp.asarray((head_of[:, None] == head_of[None, :]).astype(np.float32) / HEAD_DIM, dt)
    gqa = jnp.tile(g_qa, A_HEADS).reshape(1, A_WIDTH)
    gka = jnp.tile(g_ka, LANES // HEAD_DIM).reshape(1, LANES)

    def full(a):
        return pl.BlockSpec(a.shape, lambda i: (0,) * a.ndim)

    def rows(w):
        return pl.BlockSpec((tm, w), lambda i: (i, 0))

    args = (x1, pos, g_mix.reshape(1, d), w_a, w_w, w_b, g_cq.reshape(1, A_Q_RANK), w_uq, w_qi,
            gqa, gka, invf, hmean)
    in_specs = [rows(d), rows(1)] + [full(a) for a in args[2:]]

    def cols(w):
        return pl.BlockSpec((w, tm), lambda i: (0, i))

    def tok(w, t):
        return jax.ShapeDtypeStruct((n, w), t)

    def keymajor(w, t=dt):
        return jax.ShapeDtypeStruct((w, n), t)

    return pl.pallas_call(
        _proj_kernel,
        grid=(n // tm,),
        in_specs=in_specs,
        out_specs=[cols(A_WIDTH), cols(A_WIDTH), rows(LANES), rows(LANES), cols(V_ROWS), cols(LANES),
                   rows(2 * B_WIDTH), cols(B_WIDTH)],
        out_shape=[keymajor(A_WIDTH), keymajor(A_WIDTH), tok(LANES, dt), tok(LANES, dt), keymajor(V_ROWS),
                   keymajor(LANES, jnp.float32), tok(2 * B_WIDTH, dt), keymajor(B_WIDTH)],
        compiler_params=_cparams(("parallel",)),
        name="proj",
    )(*args)


def _stack_heads_t(qt_ref, out_ref, tq):
    row = lax.broadcasted_iota(jnp.int32, (LANES, tq), 0)
    halves = (row < HEAD_DIM, row >= HEAD_DIM)
    for h in range(qt_ref.shape[0] // HEAD_DIM):
        pair = qt_ref[(h // 2) * LANES:(h // 2 + 1) * LANES, :]
        out_ref[:, h * tq:(h + 1) * tq] = jnp.where(halves[h % 2], pair, jnp.zeros_like(pair))


def _dsa_kernel(qit_ref, wt_ref, qat_ref, ki_ref, ka_ref, vt_ref, wuvt_ref, o_ref,
                sk_ref, qis_ref, qas_ref, acc_ref, st_ref, thr_ref, excess_ref, *, tq, top_k, max_search):
    i = pl.program_id(1)
    n_chunks = i + 1
    n_search = (n_chunks * tq + SEARCH_CHUNK - 1) // SEARCH_CHUNK
    t_col = i * tq + lax.broadcasted_iota(jnp.int32, (1, tq), 1)
    nh = A_HEADS

    def keys(c, width):
        return pl.ds(pl.multiple_of(c * width, width), width)

    _stack_heads_t(qit_ref, qis_ref, tq)
    _stack_heads_t(qat_ref, qas_ref, tq)

    def score_chunk(c):
        d = _dot(ki_ref[keys(c, tq), :], qis_ref[...])
        s = jnp.zeros((tq, tq), jnp.float32)
        for h in range(IDX_HEADS):
            s = s + wt_ref[h:h + 1, :] * jnp.maximum(d[:, h * tq:(h + 1) * tq], 0.0)
        s = jnp.where(jnp.abs(s) < MIN_NORMAL, 0.0, s)
        bits = lax.bitcast_convert_type(s, jnp.int32)
        key = bits ^ ((bits >> 31) & 0x7FFFFFFF)
        s_pos = c * tq + lax.broadcasted_iota(jnp.int32, (tq, 1), 0)
        causal = s_pos <= t_col
        sk_ref[keys(c, tq), :] = jnp.where(causal, key, INT_MIN)
        top = lax.bitcast_convert_type(bits & TOP16, jnp.float32)
        st_ref[keys(c, tq), :] = jnp.where(causal, top, -jnp.inf).astype(st_ref.dtype)

    def score_body(c2, carry):
        score_chunk(2 * c2)
        score_chunk(2 * c2 + 1)
        return carry

    lax.fori_loop(0, n_search, score_body, 0)

    def tree_sum(terms):
        while len(terms) > 1:
            terms = [a + b for a, b in zip(terms[0::2], terms[1::2])] + terms[len(terms) & ~1:]
        return terms[0]

    def chunk_count(rows, thr):
        ge = jnp.where(sk_ref[rows, :] >= thr, 1.0, 0.0)
        ge = ge.reshape(SUBLANES, SEARCH_CHUNK // (SUBLANES * SUBLANES), SUBLANES, tq)
        return jnp.sum(jnp.sum(ge, axis=1), axis=0)

    def chunk_count_top(rows, thr_top):
        one, zero = jnp.ones((), st_ref.dtype), jnp.zeros((), st_ref.dtype)
        packed = 2 * SUBLANES
        ge = jnp.where(st_ref[rows, :] >= thr_top, one, zero).reshape(SEARCH_CHUNK // packed, packed, tq)
        return tree_sum([ge[r] for r in range(SEARCH_CHUNK // packed)]).astype(jnp.float32)

    def count_unrolled(chunk_fn, thr, n_static):
        part = tree_sum([chunk_fn(slice(c * SEARCH_CHUNK, (c + 1) * SEARCH_CHUNK), thr) for c in range(n_static)])
        return jnp.sum(part, axis=0, keepdims=True)

    def key_to_top(cand):
        k = cand ^ INT_MIN
        fbits = k ^ ((k >> 31) & 0x7FFF0000)
        mag = fbits & 0x7FFFFFFF
        above = jnp.where(fbits < 0, 0, MIN_NORMAL_BITS)
        fbits = jnp.where(mag == 0, fbits, jnp.where(mag < MIN_NORMAL_BITS, above, fbits))
        return lax.bitcast_convert_type(fbits, jnp.float32).astype(st_ref.dtype)

    def search(n_static):
        def top_bit_body(b, t_u):
            cand = t_u | lax.shift_left(jnp.int32(1), 31 - b)
            return jnp.where(count_unrolled(chunk_count_top, key_to_top(cand), n_static) >= top_k, cand, t_u)

        def bit_body(b, t_u):
            cand = t_u | lax.shift_left(jnp.int32(1), 31 - b)
            return jnp.where(count_unrolled(chunk_count, cand ^ INT_MIN, n_static) >= top_k, cand, t_u)

        t_u = lax.fori_loop(0, 16, top_bit_body, jnp.zeros((1, tq), jnp.int32))
        t_u = lax.fori_loop(16, 32, bit_body, t_u)
        thr_n = jnp.maximum(t_u ^ INT_MIN, INT_MIN + 1)
        thr_ref[...] = jnp.broadcast_to(thr_n, thr_ref.shape)
        excess_ref[...] = jnp.broadcast_to(count_unrolled(chunk_count, thr_n, n_static) - top_k, excess_ref.shape)

    lax.switch(n_search - 1, [functools.partial(search, n) for n in range(1, max_search + 1)])
    thr = thr_ref[0:1, :]

    excess = excess_ref[0:1, :]

    @pl.when(jnp.max(excess) > 0.0)
    def _():
        def ties_before(limit):
            def body(c, part):
                s_pos = c * SEARCH_CHUNK + lax.broadcasted_iota(jnp.int32, (SEARCH_CHUNK, 1), 0)
                hit = jnp.where(sk_ref[keys(c, SEARCH_CHUNK), :] == thr, 1.0, 0.0)
                hit = jnp.where(s_pos < limit, hit, 0.0)
                hit = hit.reshape(SUBLANES, SEARCH_CHUNK // (SUBLANES * SUBLANES), SUBLANES, tq)
                return part + jnp.sum(jnp.sum(hit, axis=1), axis=0)
            part = lax.fori_loop(0, n_search, body, jnp.zeros((SUBLANES, tq), jnp.float32))
            return jnp.sum(part, axis=0, keepdims=True)

        n_ties = ties_before(jnp.full((1, tq), n_search * SEARCH_CHUNK, jnp.int32))
        keep = n_ties - excess
        pos_bits = sk_ref.shape[0].bit_length()

        def pos_body(b, limit):
            cand = limit | lax.shift_left(jnp.int32(1), pos_bits - 1 - b)
            return jnp.where(ties_before(cand) <= keep, cand, limit)

        limit = lax.fori_loop(0, pos_bits, pos_body, jnp.zeros((1, tq), jnp.int32))

        def demote_body(c, carry):
            s_pos = c * SEARCH_CHUNK + lax.broadcasted_iota(jnp.int32, (SEARCH_CHUNK, 1), 0)
            key = sk_ref[keys(c, SEARCH_CHUNK), :]
            sk_ref[keys(c, SEARCH_CHUNK), :] = jnp.where(key == thr, jnp.where(s_pos >= limit, INT_MIN, key), key)
            return carry

        lax.fori_loop(0, n_search, demote_body, 0)

    acc_ref[...] = jnp.zeros_like(acc_ref)

    ta = SEARCH_CHUNK

    def attn_body(c, m_run):
        lg = _dot(ka_ref[keys(c, ta), :], qas_ref[...])
        bias = jnp.where(sk_ref[keys(c, ta), :] >= thr, 0.0, NEG_BIG)
        x = lg + jnp.concatenate([bias] * nh, axis=1)
        m_new = jnp.maximum(m_run, jnp.max(x, axis=0, keepdims=True))
        alpha = jnp.exp(m_run - m_new)
        p = jnp.exp(x - m_new)
        acc_ref[...] = alpha * acc_ref[...] + _dot(vt_ref[:, keys(c, ta)], p.astype(vt_ref.dtype))
        return m_new

    lax.fori_loop(0, n_search, attn_body, jnp.full((1, nh * tq), NEG_BIG, jnp.float32))

    o_lat = (acc_ref[:A_V_LATENT, :] / acc_ref[A_V_LATENT:A_V_LATENT + 1, :]).astype(wuvt_ref.dtype)
    outs = [_dot(wuvt_ref[h], o_lat[:, h * tq:(h + 1) * tq]) for h in range(nh)]
    o_ref[0] = jnp.concatenate(outs, axis=0).T.astype(o_ref.dtype)


def _dsa(qit, wit, qat, ki, ka, vat, w_uv_t, bsz, top_k, tq):
    n = qit.shape[1]
    s = n // bsz
    nq = s // tq
    assert s % SEARCH_CHUNK == 0 and SEARCH_CHUNK == 2 * tq
    dt = qit.dtype

    def qcols(w):
        return pl.BlockSpec((w, tq), lambda bi, i: (0, bi * nq + i))

    return pl.pallas_call(
        functools.partial(_dsa_kernel, tq=tq, top_k=top_k, max_search=s // SEARCH_CHUNK),
        grid=(bsz, nq),
        in_specs=[
            qcols(A_WIDTH), qcols(LANES), qcols(A_WIDTH),
            pl.BlockSpec((s, LANES), lambda bi, i: (bi, 0)),
            pl.BlockSpec((s, LANES), lambda bi, i: (bi, 0)),
            pl.BlockSpec((V_ROWS, s), lambda bi, i: (0, bi)),
            pl.BlockSpec((A_HEADS, HEAD_DIM, A_V_LATENT), lambda bi, i: (0, 0, 0)),
        ],
        out_specs=pl.BlockSpec((1, tq, A_WIDTH), lambda bi, i: (bi, i, 0)),
        out_shape=jax.ShapeDtypeStruct((bsz, s, A_WIDTH), dt),
        scratch_shapes=[pltpu.VMEM((s, tq), jnp.int32),
                        pltpu.VMEM((LANES, IDX_HEADS * tq), dt),
                        pltpu.VMEM((LANES, A_HEADS * tq), dt),
                        pltpu.VMEM((V_ROWS, A_HEADS * tq), jnp.float32),
                        pltpu.VMEM((s, tq), jnp.bfloat16),
                        pltpu.VMEM((SUBLANES, tq), jnp.int32),
                        pltpu.VMEM((SUBLANES, tq), jnp.float32)],
        compiler_params=_cparams(("parallel", "arbitrary")),
        name="dsa",
    )(qit, wit, qat, ki, ka, vat, w_uv_t)


def _stack_heads(q_ref, out_ref, tq):
    lane = lax.broadcasted_iota(jnp.int32, (tq, LANES), 1)
    halves = (lane < HEAD_DIM, lane >= HEAD_DIM)
    for h in range(q_ref.shape[-1] // HEAD_DIM):
        pair = q_ref[0, :, (h // 2) * LANES:(h // 2 + 1) * LANES]
        out_ref[h * tq:(h + 1) * tq, :] = jnp.where(halves[h % 2], pair, jnp.zeros_like(pair))


def _sb_kernel(q_ref, kt_ref, v_ref, o_ref, qs_ref, later_ref, acc_ref, *, tb):
    i = pl.program_id(1)
    row = lax.broadcasted_iota(jnp.int32, (tb, tb), 0)
    col = lax.broadcasted_iota(jnp.int32, (tb, tb), 1)
    later_ref[...] = jnp.where(row >= col, -1.0, 0.0).astype(later_ref.dtype)
    strict = col < row
    _stack_heads(q_ref, qs_ref, tb)
    acc_ref[...] = jnp.zeros_like(acc_ref)

    def block(j, carries, diag):
        ks = pl.ds(pl.multiple_of(j * tb, tb), tb)
        later = later_ref[...]
        zs, incls, out = {}, {}, [None] * B_HEADS

        def pair(h):
            return slice((h // 2) * LANES, (h // 2 + 1) * LANES)

        def logits(h):
            zs[h] = _dot(qs_ref[h * tb:(h + 1) * tb, :], kt_ref[pair(h), ks])

        def cumsum(h):
            sp = jnp.maximum(zs[h], 0.0) + jnp.log(1.0 + jnp.exp2(jnp.abs(zs[h]) * (-LOG2E)))
            if diag:
                sp = jnp.where(strict, sp, 0.0)
            incls[h] = _dot(sp.astype(later.dtype), later)

        def weights(h):
            a = jnp.exp2((zs[h] + incls[h] + carries[h]) * LOG2E)
            if diag:
                a = jnp.where(strict, a, 0.0)
            acc_ref[h] += _dot(a.astype(v_ref.dtype), v_ref[0, ks, pair(h)])
            out[h] = carries[h] + incls[h][:, 0:1]

        for step in range(B_HEADS + 2):
            if step < B_HEADS:
                logits(step)
            if 0 <= step - 1 < B_HEADS:
                cumsum(step - 1)
            if 0 <= step - 2 < B_HEADS:
                weights(step - 2)
        return tuple(out)

    carries = block(i, tuple(jnp.zeros((tb, 1), jnp.float32) for _ in range(B_HEADS)), True)
    lax.fori_loop(0, i, lambda k, c: block(i - 1 - k, c, False), carries)

    lane = lax.broadcasted_iota(jnp.int32, (tb, LANES), 1)
    for p in range(B_HEADS // 2):
        o_ref[0, :, p * LANES:(p + 1) * LANES] = jnp.where(
            lane < HEAD_DIM, acc_ref[2 * p], acc_ref[2 * p + 1]).astype(o_ref.dtype)


def _stickbreak(qv, kbt, tb):
    b, s, _ = qv.shape
    dt = qv.dtype
    return pl.pallas_call(
        functools.partial(_sb_kernel, tb=tb),
        grid=(b, s // tb),
        in_specs=[
            pl.BlockSpec((1, tb, B_WIDTH), lambda bi, i: (bi, i, 0)),
            pl.BlockSpec((B_WIDTH, s), lambda bi, i: (0, bi)),
            pl.BlockSpec((1, s, B_WIDTH), lambda bi, i: (bi, 0, 1)),
        ],
        out_specs=pl.BlockSpec((1, tb, B_WIDTH), lambda bi, i: (bi, i, 0)),
        out_shape=jax.ShapeDtypeStruct((b, s, B_WIDTH), dt),
        scratch_shapes=[pltpu.VMEM((B_HEADS * tb, LANES), dt),
                        pltpu.VMEM((tb, tb), dt),
                        pltpu.VMEM((B_HEADS, tb, LANES), jnp.float32)],
        compiler_params=_cparams(("parallel", "arbitrary")),
        name="stickbreak",
    )(qv, kbt, qv)


def _merge_kernel(x_ref, ya_ref, yb_ref, gmix_ref, wg_ref, woa_ref, wob_ref, wout_ref, o_ref):
    x = x_ref[...]
    d = x.shape[-1]
    h = _rms(x, gmix_ref[...]).astype(wg_ref.dtype)
    gates = _dot(h, wg_ref[...])
    merged = (_sigmoid(gates[:, :d]) * _dot(ya_ref[...], woa_ref[...])
              + _sigmoid(gates[:, d:]) * _dot(yb_ref[...], wob_ref[...]))
    o_ref[...] = x + _dot(merged.astype(wout_ref.dtype), wout_ref[...])


def _merge(x1, ya, yb, g_mix, w_g, w_oa, w_ob, w_out, tm):
    n, d = x1.shape

    def full(a):
        return pl.BlockSpec(a.shape, lambda i: (0,) * a.ndim)

    def rows(w):
        return pl.BlockSpec((tm, w), lambda i: (i, 0))

    args = (x1, ya, yb, g_mix.reshape(1, d), w_g, w_oa, w_ob, w_out)
    return pl.pallas_call(
        _merge_kernel,
        grid=(n // tm,),
        in_specs=[rows(d), rows(A_WIDTH), rows(B_WIDTH)] + [full(a) for a in args[3:]],
        out_specs=rows(d),
        out_shape=jax.ShapeDtypeStruct((n, d), jnp.float32),
        compiler_params=_cparams(("parallel",)),
        name="merge",
    )(*args)


def _pick(n, prefs):
    for p in prefs:
        if n % p == 0:
            return p
    return n


def _layer(x, positions, g_ffn1, w1_gate, w1_up, w1_down, g_mix, w_in, g_cq, w_uq_a, w_q_idx,
           g_q_a, g_k_a, w_uv_a, w_o_a, w_o_b, w_out, g_ffn2, w2_gate, w2_up, w2_down):
    bsz, seq, d = x.shape
    n = bsz * seq
    dt = _MXU_DT
    top_k = min(MAX_TOPK, seq // 4)
    tm_ffn = _pick(n, (2048, 1024, 512, 256))
    tf = _pick(w1_gate.shape[1], (256, 128))
    tm = _pick(n, (1024, 512, 256))
    tq = _pick(seq, (256, 128))
    tb = _pick(seq, (256, 128))

    xf = x.reshape(n, d)
    x1 = _ffn(xf, g_ffn1, w1_gate.astype(dt), w1_up.astype(dt), (0.5 * w1_down).astype(dt), tm_ffn, tf)

    o = np.cumsum((A_Q_RANK, HEAD_DIM, A_V_LATENT, IDX_DIM, IDX_HEADS, B_WIDTH, B_WIDTH, B_WIDTH, d, d))
    w_a = jnp.concatenate([w_in[:, :o[0]], w_in[:, o[0]:o[1]], w_in[:, o[2]:o[3]], w_in[:, o[1]:o[2]]],
                          axis=1).astype(dt)
    w_w = jnp.pad(w_in[:, o[3]:o[4]], ((0, 0), (0, LANES - IDX_HEADS))).astype(dt)
    w_b = jnp.concatenate([w_in[:, o[4]:o[5]] * (HEAD_DIM ** -0.5), w_in[:, o[5]:o[7]]], axis=1).astype(dt)
    w_g = w_in[:, o[7]:].astype(dt)

    qat, qit, ka, ki, vat, wit, qv, kbt = _proj(
        x1, positions.reshape(n, 1), g_mix, w_a, w_w, w_b, g_cq, w_uq_a.astype(dt), w_q_idx.astype(dt),
        g_q_a, g_k_a, tm)

    ya = _dsa(qit, wit, qat, ki, ka, vat, w_uv_a.transpose(0, 2, 1).astype(dt), bsz, top_k, tq)
    yb = _stickbreak(qv.reshape(bsz, seq, 2 * B_WIDTH), kbt, tb)

    x2 = _merge(x1, ya.reshape(n, A_WIDTH), yb.reshape(n, B_WIDTH), g_mix, w_g, w_o_a.astype(dt),
                w_o_b.astype(dt), w_out.astype(dt), tm)
    x3 = _ffn(x2, g_ffn2, w2_gate.astype(dt), w2_up.astype(dt), (0.5 * w2_down).astype(dt), tm_ffn, tf)
    return x3.reshape(bsz, seq, d)


def kernel(x, positions, g_ffn1, w1_gate, w1_up, w1_down, g_mix, w_in, g_cq, w_uq_a, w_q_idx, g_q_a, g_k_a,
           w_uv_a, w_o_a, w_o_b, w_out, g_ffn2, w2_gate, w2_up, w2_down):
    depth = g_ffn1.shape[0]
    for l in range(depth):
        x = _layer(x, positions, g_ffn1[l], w1_gate[l], w1_up[l], w1_down[l], g_mix[l], w_in[l], g_cq[l],
                   w_uq_a[l], w_q_idx[l], g_q_a[l], g_k_a[l], w_uv_a[l], w_o_a[l], w_o_b[l], w_out[l],
                   g_ffn2[l], w2_gate[l], w2_up[l], w2_down[l])
    return x
```
